```python
import math
import jax, jax.numpy as jnp
from jax import lax
import numpy as np

D_MODEL = 1024
BATCH = 16
SEQ = 2048
DEPTH = 1

N_FOX_HEADS = 8
FOX_HEAD_DIM = 64
FOX_WIDTH = N_FOX_HEADS * FOX_HEAD_DIM
N_DIFF_HEADS = 4
DIFF_QK_DIM = 64
DIFF_V_DIM = 2 * DIFF_QK_DIM
DIFF_QK_WIDTH = N_DIFF_HEADS * 2 * DIFF_QK_DIM
DIFF_WIDTH = N_DIFF_HEADS * DIFF_V_DIM
MIX_WIDTH = FOX_WIDTH + DIFF_WIDTH
IN_SIZES = (FOX_WIDTH, FOX_WIDTH, FOX_WIDTH, N_FOX_HEADS,
            DIFF_QK_WIDTH, DIFF_QK_WIDTH, DIFF_WIDTH)
IN_WIDTH = sum(IN_SIZES)
IN_SPLITS = tuple(int(v) for v in np.cumsum(IN_SIZES)[:-1])
ROPE_THETA = 500000.0
ROPE_DIM = DIFF_QK_DIM // 4
D_FF = 2816
FFN_RESIDUAL_WEIGHT = 0.5
Q_BLOCK = 128
RMS_EPS = 1e-6

kernel_name = "hybrid_fox_diffattn_macaron_layer"


def rms_norm(x, g):
    xf = x.astype(jnp.float32)
    y = xf * lax.rsqrt(jnp.mean(xf * xf, axis=-1, keepdims=True) + RMS_EPS)
    return (y * g.astype(jnp.float32)).astype(x.dtype)


def swiglu(x, w_gate, w_up, w_down):
    return (jax.nn.silu(x @ w_gate) * (x @ w_up)) @ w_down


def to_heads(t, n_heads):
    b, s, _ = t.shape
    return t.reshape(b, s, n_heads, -1).transpose(0, 2, 1, 3)


def from_heads(t):
    b, h, s, d = t.shape
    return t.transpose(0, 2, 1, 3).reshape(b, s, h * d)


def partial_rope(t, pos):
    half = ROPE_DIM // 2
    inv_freq = ROPE_THETA ** (-jnp.arange(0, ROPE_DIM, 2, dtype=jnp.float32) / ROPE_DIM)
    ang = pos.astype(jnp.float32)[:, None] * inv_freq[None, :]
    cos, sin = jnp.cos(ang), jnp.sin(ang)
    tf = t.astype(jnp.float32)
    x1, x2, rest = tf[..., :half], tf[..., half:ROPE_DIM], tf[..., ROPE_DIM:]
    out = jnp.concatenate([x1 * cos - x2 * sin, x2 * cos + x1 * sin, rest], axis=-1)
    return out.astype(t.dtype)


def causal_block_attention(q, k, v, log_decay_cum=None):
    seq = q.shape[2]
    scale = q.shape[-1] ** -0.5
    outs = []
    for i in range(seq // Q_BLOCK):
        q0, q1 = i * Q_BLOCK, (i + 1) * Q_BLOCK
        s = jnp.einsum('bhqd,bhkd->bhqk', q[:, :, q0:q1], k[:, :, :q1]).astype(jnp.float32) * scale
        if log_decay_cum is not None:
            s = s + log_decay_cum[:, :, q0:q1, None] - log_decay_cum[:, :, None, :q1]
        mask = jnp.arange(q1)[None, :] <= jnp.arange(q0, q1)[:, None]
        s = jnp.where(mask, s, jnp.finfo(jnp.float32).min)
        p = jax.nn.softmax(s, axis=-1).astype(v.dtype)
        outs.append(jnp.einsum('bhqk,bhkd->bhqd', p, v[:, :, :q1]))
    return jnp.concatenate(outs, axis=2)


def token_mixer(h, w_in, fox_forget_b, lam_q1, lam_k1, lam_q2, lam_k2, subln_g, w_out, layer):
    b, s, _ = h.shape
    pos = jnp.arange(s)
    proj = h @ w_in
    fq, fk, fv, fgate, dq, dk, dv = jnp.split(proj, IN_SPLITS, axis=-1)

    log_f = jax.nn.log_sigmoid(fgate.astype(jnp.float32) + fox_forget_b.astype(jnp.float32))
    c = jnp.cumsum(log_f, axis=1).transpose(0, 2, 1)
    fox_o = causal_block_attention(to_heads(fq, N_FOX_HEADS), to_heads(fk, N_FOX_HEADS),
                                   to_heads(fv, N_FOX_HEADS), c)
    fox_o = from_heads(fox_o)

    dq = dq.reshape(b, s, N_DIFF_HEADS, 2, DIFF_QK_DIM)
    dk = dk.reshape(b, s, N_DIFF_HEADS, 2, DIFF_QK_DIM)
    hd = lambda t: partial_rope(t.transpose(0, 2, 1, 3), pos)
    q1, q2 = hd(dq[..., 0, :]), hd(dq[..., 1, :])
    k1, k2 = hd(dk[..., 0, :]), hd(dk[..., 1, :])
    v = to_heads(dv, N_DIFF_HEADS)
    lam_init = 0.8 - 0.6 * math.exp(-0.3 * layer)
    lam = (jnp.exp(jnp.sum(lam_q1.astype(jnp.float32) * lam_k1.astype(jnp.float32)))
           - jnp.exp(jnp.sum(lam_q2.astype(jnp.float32) * lam_k2.astype(jnp.float32)))
           + lam_init)
    a1 = causal_block_attention(q1, k1, v)
    a2 = causal_block_attention(q2, k2, v)
    diff_o = a1 - lam.astype(a1.dtype) * a2
    diff_o = rms_norm(diff_o, subln_g) * (1.0 - lam_init)
    diff_o = from_heads(diff_o)

    return jnp.concatenate([fox_o, diff_o], axis=-1) @ w_out


def setup_inputs(seed: int = 0) -> dict:
    key = jax.random.key(seed)
    ks = jax.random.split(key, 24)
    L = DEPTH
    nrm = lambda k, shape, scale: jax.random.normal(k, shape, jnp.float32) * scale
    gain = lambda k, n: 1.0 + 0.02 * jax.random.normal(k, (L, n), jnp.float32)
    return {
        "x": jax.random.normal(ks[0], (BATCH, SEQ, D_MODEL), jnp.float32),
        "ffn1_pre_g": gain(ks[1], D_MODEL),
        "ffn1_post_g": gain(ks[2], D_MODEL),
        "ffn1_w_gate": nrm(ks[3], (L, D_MODEL, D_FF), D_MODEL ** -0.5),
        "ffn1_w_up": nrm(ks[4], (L, D_MODEL, D_FF), D_MODEL ** -0.5),
        "ffn1_w_down": nrm(ks[5], (L, D_FF, D_MODEL), D_FF ** -0.5),
        "mix_pre_g": gain(ks[6], D_MODEL),
        "mix_post_g": gain(ks[7], D_MODEL),
        "w_in": nrm(ks[8], (L, D_MODEL, IN_WIDTH), D_MODEL ** -0.5),
        "fox_forget_b": nrm(ks[9], (L, N_FOX_HEADS), 0.02),
        "diff_lambda_q1": nrm(ks[10], (L, DIFF_QK_DIM), 0.1),
        "diff_lambda_k1": nrm(ks[11], (L, DIFF_QK_DIM), 0.1),
        "diff_lambda_q2": nrm(ks[12], (L, DIFF_QK_DIM), 0.1),
        "diff_lambda_k2": nrm(ks[13], (L, DIFF_QK_DIM), 0.1),
        "diff_subln_g": gain(ks[14], DIFF_V_DIM),
        "w_out": nrm(ks[15], (L, MIX_WIDTH, D_MODEL), MIX_WIDTH ** -0.5),
        "ffn2_pre_g": gain(ks[16], D_MODEL),
        "ffn2_post_g": gain(ks[17], D_MODEL),
        "ffn2_w_gate": nrm(ks[18], (L, D_MODEL, D_FF), D_MODEL ** -0.5),
        "ffn2_w_up": nrm(ks[19], (L, D_MODEL, D_FF), D_MODEL ** -0.5),
        "ffn2_w_down": nrm(ks[20], (L, D_FF, D_MODEL), D_FF ** -0.5),
    }


def reference(x, ffn1_pre_g, ffn1_post_g, ffn1_w_gate, ffn1_w_up, ffn1_w_down,
              mix_pre_g, mix_post_g, w_in, fox_forget_b,
              diff_lambda_q1, diff_lambda_k1, diff_lambda_q2, diff_lambda_k2, diff_subln_g,
              w_out, ffn2_pre_g, ffn2_post_g, ffn2_w_gate, ffn2_w_up, ffn2_w_down):
    for l in range(DEPTH):
        f1 = swiglu(rms_norm(x, ffn1_pre_g[l]), ffn1_w_gate[l], ffn1_w_up[l], ffn1_w_down[l])
        x = x + FFN_RESIDUAL_WEIGHT * rms_norm(f1, ffn1_post_g[l])
        m = token_mixer(rms_norm(x, mix_pre_g[l]), w_in[l], fox_forget_b[l],
                        diff_lambda_q1[l], diff_lambda_k1[l], diff_lambda_q2[l], diff_lambda_k2[l],
                        diff_subln_g[l], w_out[l], l)
        x = x + rms_norm(m, mix_post_g[l])
        f2 = swiglu(rms_norm(x, ffn2_pre_g[l]), ffn2_w_gate[l], ffn2_w_up[l], ffn2_w_down[l])
        x = x + FFN_RESIDUAL_WEIGHT * rms_norm(f2, ffn2_post_g[l])
    return x
```

```python
import functools
import math

import jax
import jax.numpy as jnp
import numpy as np
from jax import lax
from jax.experimental import pallas as pl
from jax.experimental.pallas import tpu as pltpu

D_MODEL = 1024
N_FOX_HEADS = 8
N_DIFF_HEADS = 4
HEAD_DIM = 64
PAIR = 2 * HEAD_DIM
FOX_WIDTH = N_FOX_HEADS * HEAD_DIM
DIFF_WIDTH = N_DIFF_HEADS * PAIR
ROPE_THETA = 500000.0
ROPE_DIM = HEAD_DIM // 4
ROPE_HALF = ROPE_DIM // 2
D_FF = 2816
FFN_RESIDUAL_WEIGHT = 0.5
RMS_EPS = 1e-6
LAMBDA_INIT = 0.8 - 0.6 * math.exp(-0.3 * 0)
QK_SCALE = HEAD_DIM ** -0.5

LANES = 128
MASK_VALUE = -1e30
VMEM_LIMIT_BYTES = 56 * 1024 * 1024

PROJ_WIDTH = 3 * (FOX_WIDTH + DIFF_WIDTH)
N_UNITS = (FOX_WIDTH + DIFF_WIDTH) // PAIR
FOX_UNITS = FOX_WIDTH // PAIR


def _rms(x, g):
    return x * lax.rsqrt(jnp.mean(x * x, axis=-1, keepdims=True) + RMS_EPS) * g


def _const_spec(shape):
    return pl.BlockSpec(shape, lambda *_: (0,) * len(shape),
                        pipeline_mode=pl.Buffered(1))


FFN_TM = 512
FFN_CHUNKS = ((0, 1536), (1536, 1280))


def _ffn_kernel(x_ref, pre_g_ref, post_g_ref, wg_ref, wu_ref, wd_ref, o_ref):
    x = x_ref[...]
    xn = _rms(x, pre_g_ref[...]).astype(jnp.bfloat16)
    acc = None
    for start, width in FFN_CHUNKS:
        g = jnp.dot(xn, wg_ref[:, start:start + width],
                    preferred_element_type=jnp.float32)
        u = jnp.dot(xn, wu_ref[:, start:start + width],
                    preferred_element_type=jnp.float32)
        h = (g * jax.nn.sigmoid(g) * u).astype(jnp.bfloat16)
        part = jnp.dot(h, wd_ref[start:start + width, :],
                       preferred_element_type=jnp.float32)
        acc = part if acc is None else acc + part
    o_ref[...] = x + FFN_RESIDUAL_WEIGHT * _rms(acc, post_g_ref[...])


def _ffn(x2d, pre_g, post_g, wg, wu, wd):
    t = x2d.shape[0]
    row = pl.BlockSpec((FFN_TM, D_MODEL), lambda i: (i, 0))
    return pl.pallas_call(
        _ffn_kernel,
        grid=(t // FFN_TM,),
        in_specs=[row, _const_spec((1, D_MODEL)), _const_spec((1, D_MODEL)),
                  _const_spec((D_MODEL, D_FF)), _const_spec((D_MODEL, D_FF)),
                  _const_spec((D_FF, D_MODEL))],
        out_specs=row,
        out_shape=jax.ShapeDtypeStruct(x2d.shape, x2d.dtype),
        compiler_params=pltpu.CompilerParams(
            dimension_semantics=("arbitrary",),
            vmem_limit_bytes=VMEM_LIMIT_BYTES),
        name="ffn",
    )(x2d, pre_g, post_g, wg, wu, wd)


PROJ_TM = 512
PROJ_CHUNK = 512
ROPE_CHUNKS = (1, 3)
Q_CHUNKS = (0, 1)


def _rope_slab(y, cos, sin_up, sin_dn):
    up = pltpu.roll(y, ROPE_HALF, 1)
    dn = pltpu.roll(y, LANES - ROPE_HALF, 1)
    return y * cos + up * sin_up + dn * sin_dn


def _in_proj_kernel(x_ref, g_ref, w_ref, wgate_ref, cos_ref, sup_ref, sdn_ref,
                    proj_ref, gate_ref):
    xn = _rms(x_ref[...], g_ref[...]).astype(jnp.bfloat16)
    cos, sup, sdn = cos_ref[...], sup_ref[...], sdn_ref[...]
    for c in range(PROJ_WIDTH // PROJ_CHUNK):
        lo = c * PROJ_CHUNK
        y = jnp.dot(xn, w_ref[:, lo:lo + PROJ_CHUNK],
                    preferred_element_type=jnp.float32)
        for s in range(PROJ_CHUNK // LANES):
            slab = y[:, s * LANES:(s + 1) * LANES]
            if c in ROPE_CHUNKS:
                slab = _rope_slab(slab, cos, sup, sdn)
            if c in Q_CHUNKS:
                slab = slab * QK_SCALE
            proj_ref[:, lo + s * LANES:lo + (s + 1) * LANES] = slab.astype(
                proj_ref.dtype)
    gate_ref[...] = jnp.dot(xn, wgate_ref[...],
                            preferred_element_type=jnp.float32)


def _in_proj(x2d, pre_g, w_r, w_gate, cos, sup, sdn, seq):
    t = x2d.shape[0]
    tiles_per_seq = seq // PROJ_TM
    row = lambda w: pl.BlockSpec((PROJ_TM, w), lambda i: (i, 0))
    table = pl.BlockSpec((PROJ_TM, LANES), lambda i: (i % tiles_per_seq, 0))
    return pl.pallas_call(
        _in_proj_kernel,
        grid=(t // PROJ_TM,),
        in_specs=[row(D_MODEL), _const_spec((1, D_MODEL)),
                  _const_spec((D_MODEL, PROJ_WIDTH)),
                  _const_spec((D_MODEL, LANES)), table, table, table],
        out_specs=[row(PROJ_WIDTH), row(LANES)],
        out_shape=[jax.ShapeDtypeStruct((t, PROJ_WIDTH), jnp.bfloat16),
                   jax.ShapeDtypeStruct((t, LANES), jnp.float32)],
        compiler_params=pltpu.CompilerParams(
            dimension_semantics=("arbitrary",),
            vmem_limit_bytes=VMEM_LIMIT_BYTES),
        name="in_proj",
    )(x2d, pre_g, w_r, w_gate, cos, sup, sdn)


def _decay_kernel(gate_ref, b_ref, ccol_ref, crow_ref):
    z = gate_ref[0] + b_ref[...]
    c = jnp.minimum(z, 0.0) - jnp.log1p(jnp.exp(-jnp.abs(z)))
    seq = c.shape[0]
    rows = lax.broadcasted_iota(jnp.int32, c.shape, 0)
    shift = 1
    while shift < seq:
        c = c + jnp.where(rows >= shift, pltpu.roll(c, shift, 0), 0.0)
        shift *= 2
    ccol_ref[0] = c
    crow_ref[0] = c.T[:N_FOX_HEADS, :]


def _decay(gate, bias_row, batch, seq):
    gate3 = gate.reshape(batch, seq, LANES)
    return pl.pallas_call(
        _decay_kernel,
        grid=(batch,),
        in_specs=[pl.BlockSpec((1, seq, LANES), lambda b: (b, 0, 0)),
                  _const_spec((1, LANES))],
        out_specs=[pl.BlockSpec((1, seq, LANES), lambda b: (b, 0, 0)),
                   pl.BlockSpec((1, N_FOX_HEADS, seq), lambda b: (b, 0, 0))],
        out_shape=[jax.ShapeDtypeStruct((batch, seq, LANES), jnp.float32),
                   jax.ShapeDtypeStruct((batch, N_FOX_HEADS, seq), jnp.float32)],
        compiler_params=pltpu.CompilerParams(
            dimension_semantics=("arbitrary",),
            vmem_limit_bytes=VMEM_LIMIT_BYTES),
        name="decay",
    )(gate3, bias_row)


ATT_TQ = 256
ATT_TK = 256


def _attn_body(is_fox, q_ref, k_ref, v_ref, ccol_ref, crow_ref, lam_ref, g_ref,
               o_ref, m_scr, l_scr, acc_scr):
    tq, tk = ATT_TQ, ATT_TK
    unit = pl.program_id(1)
    qi = pl.program_id(2)

    q = q_ref[...]
    lane = lax.broadcasted_iota(jnp.int32, (tq, LANES), 1)
    zero = jnp.zeros_like(q)
    q2 = jnp.concatenate([jnp.where(lane < HEAD_DIM, q, zero),
                          jnp.where(lane >= HEAD_DIM, q, zero)], axis=0)

    if is_fox:
        ccol = ccol_ref[0]
        head_a = 2 * unit
        pick = lambda h: jnp.sum(jnp.where(lane == h, ccol, 0.0), axis=1,
                                 keepdims=True)
        ct = jnp.concatenate([pick(head_a), pick(head_a + 1)], axis=0)
        ct = jnp.broadcast_to(ct, (2 * tq, LANES))

    m_scr[...] = jnp.full_like(m_scr, MASK_VALUE)
    l_scr[...] = jnp.zeros_like(l_scr)
    acc_scr[...] = jnp.zeros_like(acc_scr)

    def step(j, masked):
        k0 = pl.multiple_of(j * tk, tk)
        kb = k_ref[pl.ds(k0, tk), :]
        vb = v_ref[pl.ds(k0, tk), :]
        s = lax.dot_general(q2, kb, (((1,), (1,)), ((), ())),
                            preferred_element_type=jnp.float32)
        if is_fox:
            row_of = lambda h: crow_ref[0, pl.ds(h, 1), pl.ds(k0, tk)]
            s = jnp.concatenate([s[:tq] - row_of(head_a),
                                 s[tq:] - row_of(head_a + 1)], axis=0)
        if masked:
            r = lax.broadcasted_iota(jnp.int32, (2 * tq, tk), 0) & (tq - 1)
            c = lax.broadcasted_iota(jnp.int32, (2 * tq, tk), 1)
            s = jnp.where(r >= c, s, MASK_VALUE)
        m_prev = m_scr[...]
        row_max = jnp.max(s, axis=1, keepdims=True)
        if is_fox:
            m_next = jnp.maximum(m_prev, row_max + ct)
            offset = m_next - ct
        else:
            m_next = jnp.maximum(m_prev, row_max)
            offset = m_next
        p = jnp.exp(s - jnp.tile(offset, (1, tk // LANES)))
        alpha = jnp.exp(m_prev - m_next)
        l_scr[...] = alpha * l_scr[...] + jnp.sum(p, axis=1, keepdims=True)
        acc_scr[...] = alpha * acc_scr[...] + jnp.dot(
            p.astype(vb.dtype), vb, preferred_element_type=jnp.float32)
        m_scr[...] = m_next

    def full_step(j, carry):
        step(j, masked=False)
        return carry

    lax.fori_loop(0, qi, full_step, 0)
    step(qi, masked=True)

    o = acc_scr[...] / l_scr[...]
    oa, ob = o[:tq], o[tq:]
    if is_fox:
        out = jnp.where(lane < HEAD_DIM, oa, ob)
    else:
        lam_rows = lam_ref[...]
        dot = lambda i: jnp.sum(lam_rows[i:i + 1] * lam_rows[i + 1:i + 2],
                                axis=1, keepdims=True)
        lam = jnp.exp(dot(0)) - jnp.exp(dot(2)) + LAMBDA_INIT
        d = oa - lam * ob
        out = _rms(d, g_ref[...]) * (1.0 - LAMBDA_INIT)
    o_ref[...] = out.astype(o_ref.dtype)


def _fox_kernel(q_ref, k_ref, v_ref, ccol_ref, crow_ref, o_ref, m, l, acc):
    _attn_body(True, q_ref, k_ref, v_ref, ccol_ref, crow_ref, None, None,
               o_ref, m, l, acc)


def _diff_kernel(q_ref, k_ref, v_ref, lam_ref, g_ref, o_ref, m, l, acc):
    _attn_body(False, q_ref, k_ref, v_ref, None, None, lam_ref, g_ref,
               o_ref, m, l, acc)


def _attention(proj, batch, seq, first_unit, n_units, kernel, extra_specs,
               extra_args, name):
    nq = seq // ATT_TQ
    q_spec = pl.BlockSpec((ATT_TQ, PAIR),
                          lambda b, u, i: (b * nq + i, first_unit + u))
    kv_spec = lambda off: pl.BlockSpec(
        (seq, PAIR), lambda b, u, i: (b, off + first_unit + u))
    out_spec = pl.BlockSpec((ATT_TQ, PAIR), lambda b, u, i: (b * nq + i, u))
    stat = pltpu.VMEM((2 * ATT_TQ, LANES), jnp.float32)
    return pl.pallas_call(
        kernel,
        grid=(batch, n_units, nq),
        in_specs=[q_spec, kv_spec(N_UNITS), kv_spec(2 * N_UNITS)] + extra_specs,
        out_specs=out_spec,
        out_shape=jax.ShapeDtypeStruct((batch * seq, n_units * PAIR),
                                       jnp.bfloat16),
        scratch_shapes=[stat, stat, stat],
        compiler_params=pltpu.CompilerParams(
            dimension_semantics=("arbitrary", "arbitrary", "arbitrary"),
            vmem_limit_bytes=VMEM_LIMIT_BYTES),
        name=name,
    )(proj, proj, proj, *extra_args)


OUT_TM = 512


def _out_proj_kernel(x_ref, fox_ref, diff_ref, w_ref, g_ref, o_ref):
    m = jnp.dot(fox_ref[...], w_ref[:FOX_WIDTH, :],
                preferred_element_type=jnp.float32)
    m = m + jnp.dot(diff_ref[...], w_ref[FOX_WIDTH:, :],
                    preferred_element_type=jnp.float32)
    o_ref[...] = x_ref[...] + _rms(m, g_ref[...])


def _out_proj(x2d, fox_o, diff_o, w_out, post_g):
    t = x2d.shape[0]
    row = lambda w: pl.BlockSpec((OUT_TM, w), lambda i: (i, 0))
    return pl.pallas_call(
        _out_proj_kernel,
        grid=(t // OUT_TM,),
        in_specs=[row(D_MODEL), row(FOX_WIDTH), row(DIFF_WIDTH),
                  _const_spec((FOX_WIDTH + DIFF_WIDTH, D_MODEL)),
                  _const_spec((1, D_MODEL))],
        out_specs=row(D_MODEL),
        out_shape=jax.ShapeDtypeStruct(x2d.shape, x2d.dtype),
        compiler_params=pltpu.CompilerParams(
            dimension_semantics=("arbitrary",),
            vmem_limit_bytes=VMEM_LIMIT_BYTES),
        name="out_proj",
    )(x2d, fox_o, diff_o, w_out, post_g)


def _rope_tables(seq):
    inv_freq = ROPE_THETA ** (-jnp.arange(0, ROPE_DIM, 2, dtype=jnp.float32)
                              / ROPE_DIM)
    ang = jnp.arange(seq).astype(jnp.float32)[:, None] * inv_freq[None, :]
    cos, sin = jnp.cos(ang), jnp.sin(ang)
    ones = jnp.ones((seq, HEAD_DIM - ROPE_DIM), jnp.float32)
    zeros = jnp.zeros((seq, HEAD_DIM - ROPE_DIM), jnp.float32)
    z8 = jnp.zeros((seq, ROPE_HALF), jnp.float32)
    cos64 = jnp.concatenate([cos, cos, ones], axis=1)
    up64 = jnp.concatenate([z8, sin, zeros], axis=1)
    dn64 = jnp.concatenate([-sin, z8, zeros], axis=1)
    two = lambda a: jnp.concatenate([a, a], axis=1)
    return two(cos64), two(up64), two(dn64)


def _relayout_w_in(w_in):
    fw, dw = FOX_WIDTH, DIFF_WIDTH
    edges = np.cumsum([0, fw, fw, fw, N_FOX_HEADS, dw, dw, dw])
    fq, fk, fv, fgate, dq, dk, dv = (
        w_in[:, edges[i]:edges[i + 1]] for i in range(7))
    w_r = jnp.concatenate([fq, dq, fk, dk, fv, dv], axis=1)
    w_gate = jnp.pad(fgate, ((0, 0), (0, LANES - N_FOX_HEADS)))
    return w_r.astype(jnp.bfloat16), w_gate.astype(jnp.bfloat16)


def kernel(x, ffn1_pre_g, ffn1_post_g, ffn1_w_gate, ffn1_w_up, ffn1_w_down, mix_pre_g, mix_post_g, w_in, fox_forget_b, diff_lambda_q1, diff_lambda_k1, diff_lambda_q2, diff_lambda_k2, diff_subln_g, w_out, ffn2_pre_g, ffn2_post_g, ffn2_w_gate, ffn2_w_up, ffn2_w_down):
    batch, seq, d = x.shape
    bf = lambda w: w.astype(jnp.bfloat16)
    x2d = x.reshape(batch * seq, d)
    cos, sup, sdn = _rope_tables(seq)
    for l in range(ffn1_pre_g.shape[0]):
        x2d = _ffn(x2d, ffn1_pre_g[l][None], ffn1_post_g[l][None],
                   bf(ffn1_w_gate[l]), bf(ffn1_w_up[l]), bf(ffn1_w_down[l]))

        w_r, w_gate = _relayout_w_in(w_in[l])
        proj, gate = _in_proj(x2d, mix_pre_g[l][None], w_r, w_gate,
                              cos, sup, sdn, seq)
        bias_row = jnp.pad(fox_forget_b[l], (0, LANES - N_FOX_HEADS))[None]
        ccol, crow = _decay(gate, bias_row, batch, seq)

        nq = seq // ATT_TQ
        fox_o = _attention(
            proj, batch, seq, 0, FOX_UNITS, _fox_kernel,
            [pl.BlockSpec((1, ATT_TQ, LANES), lambda b, u, i: (b, i, 0)),
             pl.BlockSpec((1, N_FOX_HEADS, seq), lambda b, u, i: (b, 0, 0))],
            [ccol, crow], "fox_attn")
        lam_rows = jnp.pad(
            jnp.stack([diff_lambda_q1[l], diff_lambda_k1[l],
                       diff_lambda_q2[l], diff_lambda_k2[l]]),
            ((0, 4), (0, LANES - HEAD_DIM)))
        diff_o = _attention(
            proj, batch, seq, FOX_UNITS, N_UNITS - FOX_UNITS, _diff_kernel,
            [_const_spec((8, LANES)), _const_spec((1, PAIR))],
            [lam_rows, diff_subln_g[l][None]], "diff_attn")

        x2d = _out_proj(x2d, fox_o, diff_o, bf(w_out[l]), mix_post_g[l][None])

        x2d = _ffn(x2d, ffn2_pre_g[l][None], ffn2_post_g[l][None],
                   bf(ffn2_w_gate[l]), bf(ffn2_w_up[l]), bf(ffn2_w_down[l]))
    return x2d.reshape(batch, seq, d)
```

```python
import math

import jax
import jax.numpy as jnp
import numpy as np
from jax import lax
from jax.experimental import pallas as pl
from jax.experimental.pallas import tpu as pltpu

D_MODEL = 1024
N_FOX_HEADS = 8
N_DIFF_HEADS = 4
HEAD_DIM = 64
PAIR = 2 * HEAD_DIM
FOX_WIDTH = N_FOX_HEADS * HEAD_DIM
DIFF_WIDTH = N_DIFF_HEADS * PAIR
ROPE_THETA = 500000.0
ROPE_DIM = HEAD_DIM // 4
ROPE_HALF = ROPE_DIM // 2
D_FF = 2816
FFN_RESIDUAL_WEIGHT = 0.5
RMS_EPS = 1e-6
LAMBDA_INIT = 0.8 - 0.6 * math.exp(-0.3 * 0)
QK_SCALE = HEAD_DIM ** -0.5

LANES = 128
MASK_VALUE = -1e30
VMEM_LIMIT_BYTES = 56 * 1024 * 1024

GROUP = FOX_WIDTH
PROJ_WIDTH = 6 * GROUP
UNITS = GROUP // PAIR


def _rms(x, g):
    return x * lax.rsqrt(jnp.mean(x * x, axis=-1, keepdims=True) + RMS_EPS) * g


def _const_spec(shape):
    return pl.BlockSpec(shape, lambda *_: (0,) * len(shape),
                        pipeline_mode=pl.Buffered(1))


FFN_TM = 512
FFN_CHUNKS = ((0, 1536), (1536, 1280))


def _ffn_kernel(x_ref, pre_g_ref, post_g_ref, wg_ref, wu_ref, wd_ref, o_ref):
    x = x_ref[...]
    xn = _rms(x, pre_g_ref[...]).astype(jnp.bfloat16)
    acc = None
    for start, width in FFN_CHUNKS:
        g = jnp.dot(xn, wg_ref[:, start:start + width],
                    preferred_element_type=jnp.float32)
        u = jnp.dot(xn, wu_ref[:, start:start + width],
                    preferred_element_type=jnp.float32)
        h = (g * jax.nn.sigmoid(g) * u).astype(jnp.bfloat16)
        part = jnp.dot(h, wd_ref[start:start + width, :],
                       preferred_element_type=jnp.float32)
        acc = part if acc is None else acc + part
    o_ref[...] = x + FFN_RESIDUAL_WEIGHT * _rms(acc, post_g_ref[...])


def _ffn(x2d, pre_g, post_g, wg, wu, wd):
    t = x2d.shape[0]
    row = pl.BlockSpec((FFN_TM, D_MODEL), lambda i: (i, 0))
    return pl.pallas_call(
        _ffn_kernel,
        grid=(t // FFN_TM,),
        in_specs=[row, _const_spec((1, D_MODEL)), _const_spec((1, D_MODEL)),
                  _const_spec((D_MODEL, D_FF)), _const_spec((D_MODEL, D_FF)),
                  _const_spec((D_FF, D_MODEL))],
        out_specs=row,
        out_shape=jax.ShapeDtypeStruct(x2d.shape, x2d.dtype),
        compiler_params=pltpu.CompilerParams(
            dimension_semantics=("arbitrary",),
            vmem_limit_bytes=VMEM_LIMIT_BYTES),
        name="ffn",
    )(x2d, pre_g, post_g, wg, wu, wd)


PROJ_TM = 512
ROPE_GROUPS = (1, 3)
Q_GROUPS = (0, 1)


def _rope_slab(y, cos, sin_up, sin_dn):
    up = pltpu.roll(y, ROPE_HALF, 1)
    dn = pltpu.roll(y, LANES - ROPE_HALF, 1)
    return y * cos + up * sin_up + dn * sin_dn


def _in_proj_kernel(x_ref, g_ref, w_ref, wgate_ref, cos_ref, sup_ref, sdn_ref,
                    proj_ref, gate_ref):
    xn = _rms(x_ref[...], g_ref[...]).astype(jnp.bfloat16)
    cos, sup, sdn = cos_ref[...], sup_ref[...], sdn_ref[...]
    for c in range(PROJ_WIDTH // GROUP):
        lo = c * GROUP
        y = jnp.dot(xn, w_ref[:, lo:lo + GROUP],
                    preferred_element_type=jnp.float32)
        for s in range(UNITS):
            slab = y[:, s * LANES:(s + 1) * LANES]
            if c in ROPE_GROUPS:
                slab = _rope_slab(slab, cos, sup, sdn)
            if c in Q_GROUPS:
                slab = slab * QK_SCALE
            proj_ref[:, lo + s * LANES:lo + (s + 1) * LANES] = slab.astype(
                proj_ref.dtype)
    gate_ref[...] = jnp.dot(xn, wgate_ref[...],
                            preferred_element_type=jnp.float32)


def _in_proj(x2d, pre_g, w_r, w_gate, cos, sup, sdn, seq):
    t = x2d.shape[0]
    tiles_per_seq = seq // PROJ_TM
    row = lambda w: pl.BlockSpec((PROJ_TM, w), lambda i: (i, 0))
    table = pl.BlockSpec((PROJ_TM, LANES), lambda i: (i % tiles_per_seq, 0))
    return pl.pallas_call(
        _in_proj_kernel,
        grid=(t // PROJ_TM,),
        in_specs=[row(D_MODEL), _const_spec((1, D_MODEL)),
                  _const_spec((D_MODEL, PROJ_WIDTH)),
                  _const_spec((D_MODEL, LANES)), table, table, table],
        out_specs=[row(PROJ_WIDTH), row(LANES)],
        out_shape=[jax.ShapeDtypeStruct((t, PROJ_WIDTH), jnp.bfloat16),
                   jax.ShapeDtypeStruct((t, LANES), jnp.float32)],
        compiler_params=pltpu.CompilerParams(
            dimension_semantics=("arbitrary",),
            vmem_limit_bytes=VMEM_LIMIT_BYTES),
        name="in_proj",
    )(x2d, pre_g, w_r, w_gate, cos, sup, sdn)


def _decay_kernel(gate_ref, b_ref, ccol_ref, crow_ref):
    z = gate_ref[0] + b_ref[...]
    c = jnp.minimum(z, 0.0) - jnp.log1p(jnp.exp(-jnp.abs(z)))
    seq = c.shape[0]
    rows = lax.broadcasted_iota(jnp.int32, c.shape, 0)
    shift = 1
    while shift < seq:
        c = c + jnp.where(rows >= shift, pltpu.roll(c, shift, 0), 0.0)
        shift *= 2
    ccol_ref[0] = c
    crow_ref[0] = c.T[:N_FOX_HEADS, :]


def _decay(gate, bias_row, batch, seq):
    gate3 = gate.reshape(batch, seq, LANES)
    return pl.pallas_call(
        _decay_kernel,
        grid=(batch,),
        in_specs=[pl.BlockSpec((1, seq, LANES), lambda b: (b, 0, 0)),
                  _const_spec((1, LANES))],
        out_specs=[pl.BlockSpec((1, seq, LANES), lambda b: (b, 0, 0)),
                   pl.BlockSpec((1, N_FOX_HEADS, seq), lambda b: (b, 0, 0))],
        out_shape=[jax.ShapeDtypeStruct((batch, seq, LANES), jnp.float32),
                   jax.ShapeDtypeStruct((batch, N_FOX_HEADS, seq), jnp.float32)],
        compiler_params=pltpu.CompilerParams(
            dimension_semantics=("arbitrary",),
            vmem_limit_bytes=VMEM_LIMIT_BYTES),
        name="decay",
    )(gate3, bias_row)


ATT_TQ = 256
ATT_TK = 256


def _attn_body(is_fox, q_ref, k_ref, v_ref, ccol_ref, crow_ref, lam_ref, g_ref,
               o_ref, m_scr, l_scr, acc_scr, crep_scr):
    tq, tk = ATT_TQ, ATT_TK
    qi = pl.program_id(1)
    q0 = pl.multiple_of(qi * tq, tq)

    if is_fox:
        @pl.when(qi == 0)
        def _():
            ccol = ccol_ref[0]
            for h in range(N_FOX_HEADS):
                crep_scr[h] = jnp.broadcast_to(ccol[:, h:h + 1], ccol.shape)

    lane = lax.broadcasted_iota(jnp.int32, (tq, LANES), 1)
    q2, ct = [], []
    for u in range(UNITS):
        q = q_ref[:, u * PAIR:(u + 1) * PAIR]
        zero = jnp.zeros_like(q)
        q2.append(jnp.concatenate([jnp.where(lane < HEAD_DIM, q, zero),
                                   jnp.where(lane >= HEAD_DIM, q, zero)],
                                  axis=0))
        if is_fox:
            ct.append(jnp.concatenate(
                [crow_ref[0, 2 * u:2 * u + 1, pl.ds(q0, tq)],
                 crow_ref[0, 2 * u + 1:2 * u + 2, pl.ds(q0, tq)]], axis=1))

    m_scr[...] = jnp.full_like(m_scr, MASK_VALUE)
    l_scr[...] = jnp.zeros_like(l_scr)
    acc_scr[...] = jnp.zeros_like(acc_scr)

    def step(j, masked):
        k0 = pl.multiple_of(j * tk, tk)

        def logits(u):
            kb = k_ref[pl.ds(k0, tk), u * PAIR:(u + 1) * PAIR]
            return lax.dot_general(kb, q2[u], (((1,), (1,)), ((), ())),
                                   preferred_element_type=jnp.float32)

        s_next = logits(0)
        for u in range(UNITS):
            s = s_next
            if u + 1 < UNITS:
                s_next = logits(u + 1)
            vb = v_ref[pl.ds(k0, tk), u * PAIR:(u + 1) * PAIR]
            if is_fox:
                rep = lambda h: jnp.tile(crep_scr[h, pl.ds(k0, tk), :],
                                         (1, tq // LANES))
                s = jnp.concatenate([s[:, :tq] - rep(2 * u),
                                     s[:, tq:] - rep(2 * u + 1)], axis=1)
            if masked:
                r = lax.broadcasted_iota(jnp.int32, (tk, 2 * tq), 0)
                c = lax.broadcasted_iota(jnp.int32, (tk, 2 * tq), 1) & (tq - 1)
                s = jnp.where(r <= c, s, MASK_VALUE)
            m_prev = m_scr[u]
            col_max = jnp.max(s, axis=0, keepdims=True)
            if is_fox:
                m_next = jnp.maximum(m_prev, col_max + ct[u])
                offset = m_next - ct[u]
            else:
                m_next = jnp.maximum(m_prev, col_max)
                offset = m_next
            p = jnp.exp(s - offset)
            alpha = jnp.exp(m_prev - m_next)
            l_scr[u] = alpha * l_scr[u] + jnp.sum(p, axis=0, keepdims=True)
            pv = lax.dot_general(vb, p.astype(vb.dtype),
                                 (((0,), (0,)), ((), ())),
                                 preferred_element_type=jnp.float32)
            acc_scr[u] = alpha * acc_scr[u] + pv
            m_scr[u] = m_next

    def full_step(j, carry):
        step(j, masked=False)
        return carry

    lax.fori_loop(0, qi, full_step, 0)
    step(qi, masked=True)

    if not is_fox:
        lam_rows = lam_ref[...]
        dot = lambda i: jnp.sum(lam_rows[i:i + 1] * lam_rows[i + 1:i + 2],
                                axis=1, keepdims=True)
        lam = jnp.exp(dot(0)) - jnp.exp(dot(2)) + LAMBDA_INIT
    for u in range(UNITS):
        o_t = acc_scr[u] / l_scr[u]
        if is_fox:
            out = jnp.concatenate([o_t[:HEAD_DIM, :tq], o_t[HEAD_DIM:, tq:]],
                                  axis=0).T
        else:
            d_t = o_t[:, :tq] - lam * o_t[:, tq:]
            inv = lax.rsqrt(jnp.mean(d_t * d_t, axis=0, keepdims=True) + RMS_EPS)
            out = (d_t * inv).T * g_ref[...] * (1.0 - LAMBDA_INIT)
        o_ref[:, u * PAIR:(u + 1) * PAIR] = out.astype(o_ref.dtype)


def _fox_kernel(q_ref, k_ref, v_ref, ccol_ref, crow_ref, o_ref, m, l, acc, crep):
    _attn_body(True, q_ref, k_ref, v_ref, ccol_ref, crow_ref, None, None,
               o_ref, m, l, acc, crep)


def _diff_kernel(q_ref, k_ref, v_ref, lam_ref, g_ref, o_ref, m, l, acc):
    _attn_body(False, q_ref, k_ref, v_ref, None, None, lam_ref, g_ref,
               o_ref, m, l, acc, None)


def _attention(proj, batch, seq, group, kernel, extra_specs, extra_args,
               extra_scratch, name):
    nq = seq // ATT_TQ
    q_spec = pl.BlockSpec((ATT_TQ, GROUP), lambda b, i: (b * nq + i, group))
    kv_spec = lambda g: pl.BlockSpec((seq, GROUP), lambda b, i: (b, g))
    out_spec = pl.BlockSpec((ATT_TQ, GROUP), lambda b, i: (b * nq + i, 0))
    stat = pltpu.VMEM((UNITS, 1, 2 * ATT_TQ), jnp.float32)
    acc = pltpu.VMEM((UNITS, PAIR, 2 * ATT_TQ), jnp.float32)
    return pl.pallas_call(
        kernel,
        grid=(batch, nq),
        in_specs=[q_spec, kv_spec(2 + group), kv_spec(4 + group)] + extra_specs,
        out_specs=out_spec,
        out_shape=jax.ShapeDtypeStruct((batch * seq, GROUP), jnp.bfloat16),
        scratch_shapes=[stat, stat, acc] + extra_scratch,
        compiler_params=pltpu.CompilerParams(
            dimension_semantics=("arbitrary", "arbitrary"),
            vmem_limit_bytes=VMEM_LIMIT_BYTES),
        name=name,
    )(proj, proj, proj, *extra_args)


OUT_TM = 512


def _out_proj_kernel(x_ref, fox_ref, diff_ref, w_ref, g_ref, o_ref):
    m = jnp.dot(fox_ref[...], w_ref[:FOX_WIDTH, :],
                preferred_element_type=jnp.float32)
    m = m + jnp.dot(diff_ref[...], w_ref[FOX_WIDTH:, :],
                    preferred_element_type=jnp.float32)
    o_ref[...] = x_ref[...] + _rms(m, g_ref[...])


def _out_proj(x2d, fox_o, diff_o, w_out, post_g):
    t = x2d.shape[0]
    row = lambda w: pl.BlockSpec((OUT_TM, w), lambda i: (i, 0))
    return pl.pallas_call(
        _out_proj_kernel,
        grid=(t // OUT_TM,),
        in_specs=[row(D_MODEL), row(FOX_WIDTH), row(DIFF_WIDTH),
                  _const_spec((FOX_WIDTH + DIFF_WIDTH, D_MODEL)),
                  _const_spec((1, D_MODEL))],
        out_specs=row(D_MODEL),
        out_shape=jax.ShapeDtypeStruct(x2d.shape, x2d.dtype),
        compiler_params=pltpu.CompilerParams(
            dimension_semantics=("arbitrary",),
            vmem_limit_bytes=VMEM_LIMIT_BYTES),
        name="out_proj",
    )(x2d, fox_o, diff_o, w_out, post_g)


def _rope_tables(seq):
    inv_freq = ROPE_THETA ** (-jnp.arange(0, ROPE_DIM, 2, dtype=jnp.float32)
                              / ROPE_DIM)
    ang = jnp.arange(seq).astype(jnp.float32)[:, None] * inv_freq[None, :]
    cos, sin = jnp.cos(ang), jnp.sin(ang)
    ones = jnp.ones((seq, HEAD_DIM - ROPE_DIM), jnp.float32)
    zeros = jnp.zeros((seq, HEAD_DIM - ROPE_DIM), jnp.float32)
    z8 = jnp.zeros((seq, ROPE_HALF), jnp.float32)
    cos64 = jnp.concatenate([cos, cos, ones], axis=1)
    up64 = jnp.concatenate([z8, sin, zeros], axis=1)
    dn64 = jnp.concatenate([-sin, z8, zeros], axis=1)
    two = lambda a: jnp.concatenate([a, a], axis=1)
    return two(cos64), two(up64), two(dn64)


def _relayout_w_in(w_in):
    fw, dw = FOX_WIDTH, DIFF_WIDTH
    edges = np.cumsum([0, fw, fw, fw, N_FOX_HEADS, dw, dw, dw])
    fq, fk, fv, fgate, dq, dk, dv = (
        w_in[:, edges[i]:edges[i + 1]] for i in range(7))
    w_r = jnp.concatenate([fq, dq, fk, dk, fv, dv], axis=1)
    w_gate = jnp.pad(fgate, ((0, 0), (0, LANES - N_FOX_HEADS)))
    return w_r.astype(jnp.bfloat16), w_gate.astype(jnp.bfloat16)


def kernel(x, ffn1_pre_g, ffn1_post_g, ffn1_w_gate, ffn1_w_up, ffn1_w_down, mix_pre_g, mix_post_g, w_in, fox_forget_b, diff_lambda_q1, diff_lambda_k1, diff_lambda_q2, diff_lambda_k2, diff_subln_g, w_out, ffn2_pre_g, ffn2_post_g, ffn2_w_gate, ffn2_w_up, ffn2_w_down):
    batch, seq, d = x.shape
    bf = lambda w: w.astype(jnp.bfloat16)
    x2d = x.reshape(batch * seq, d)
    cos, sup, sdn = _rope_tables(seq)
    for l in range(ffn1_pre_g.shape[0]):
        x2d = _ffn(x2d, ffn1_pre_g[l][None], ffn1_post_g[l][None],
                   bf(ffn1_w_gate[l]), bf(ffn1_w_up[l]), bf(ffn1_w_down[l]))

        w_r, w_gate = _relayout_w_in(w_in[l])
        proj, gate = _in_proj(x2d, mix_pre_g[l][None], w_r, w_gate,
                              cos, sup, sdn, seq)
        bias_row = jnp.pad(fox_forget_b[l], (0, LANES - N_FOX_HEADS))[None]
        ccol, crow = _decay(gate, bias_row, batch, seq)

        fox_o = _attention(
            proj, batch, seq, 0, _fox_kernel,
            [pl.BlockSpec((1, seq, LANES), lambda b, i: (b, 0, 0)),
             pl.BlockSpec((1, N_FOX_HEADS, seq), lambda b, i: (b, 0, 0))],
            [ccol, crow],
            [pltpu.VMEM((N_FOX_HEADS, seq, LANES), jnp.float32)], "fox_attn")
        lam_rows = jnp.pad(
            jnp.stack([diff_lambda_q1[l], diff_lambda_k1[l],
                       diff_lambda_q2[l], diff_lambda_k2[l]]),
            ((0, 4), (0, LANES - HEAD_DIM)))
        diff_o = _attention(
            proj, batch, seq, 1, _diff_kernel,
            [_const_spec((8, LANES)), _const_spec((1, PAIR))],
            [lam_rows, diff_subln_g[l][None]], [], "diff_attn")

        x2d = _out_proj(x2d, fox_o, diff_o, bf(w_out[l]), mix_post_g[l][None])

        x2d = _ffn(x2d, ffn2_pre_g[l][None], ffn2_post_g[l][None],
                   bf(ffn2_w_gate[l]), bf(ffn2_w_up[l]), bf(ffn2_w_down[l]))
    return x2d.reshape(batch, seq, d)
```

```python
import math

import jax
import jax.numpy as jnp
import numpy as np
from jax import lax
from jax.experimental import pallas as pl
from jax.experimental.pallas import tpu as pltpu

D_MODEL = 1024
N_FOX_HEADS = 8
N_DIFF_HEADS = 4
HEAD_DIM = 64
PAIR = 2 * HEAD_DIM
FOX_WIDTH = N_FOX_HEADS * HEAD_DIM
DIFF_WIDTH = N_DIFF_HEADS * PAIR
ROPE_THETA = 500000.0
ROPE_DIM = HEAD_DIM // 4
ROPE_HALF = ROPE_DIM // 2
D_FF = 2816
FFN_RESIDUAL_WEIGHT = 0.5
RMS_EPS = 1e-6
LAMBDA_INIT = 0.8 - 0.6 * math.exp(-0.3 * 0)
QK_SCALE = HEAD_DIM ** -0.5
LOG2E = math.log2(math.e)

LANES = 128
MASK_VALUE = -1e30
VMEM_LIMIT_BYTES = 56 * 1024 * 1024

GROUP = FOX_WIDTH
PROJ_WIDTH = 6 * GROUP
UNITS = GROUP // PAIR


def _rms(x, g):
    return x * lax.rsqrt(jnp.mean(x * x, axis=-1, keepdims=True) + RMS_EPS) * g


def _const_spec(shape):
    return pl.BlockSpec(shape, lambda *_: (0,) * len(shape),
                        pipeline_mode=pl.Buffered(1))


FFN_TM = 512
FFN_CHUNKS = ((0, 1536), (1536, 1280))


def _ffn_kernel(x_ref, pre_g_ref, post_g_ref, wg_ref, wu_ref, wd_ref, o_ref):
    x = x_ref[...]
    xn = _rms(x, pre_g_ref[...]).astype(jnp.bfloat16)
    acc = None
    for start, width in FFN_CHUNKS:
        g = jnp.dot(xn, wg_ref[:, start:start + width],
                    preferred_element_type=jnp.float32)
        u = jnp.dot(xn, wu_ref[:, start:start + width],
                    preferred_element_type=jnp.float32)
        h = (g * jax.nn.sigmoid(g) * u).astype(jnp.bfloat16)
        part = jnp.dot(h, wd_ref[start:start + width, :],
                       preferred_element_type=jnp.float32)
        acc = part if acc is None else acc + part
    o_ref[...] = x + FFN_RESIDUAL_WEIGHT * _rms(acc, post_g_ref[...])


def _ffn(x2d, pre_g, post_g, wg, wu, wd):
    t = x2d.shape[0]
    row = pl.BlockSpec((FFN_TM, D_MODEL), lambda i: (i, 0))
    return pl.pallas_call(
        _ffn_kernel,
        grid=(t // FFN_TM,),
        in_specs=[row, _const_spec((1, D_MODEL)), _const_spec((1, D_MODEL)),
                  _const_spec((D_MODEL, D_FF)), _const_spec((D_MODEL, D_FF)),
                  _const_spec((D_FF, D_MODEL))],
        out_specs=row,
        out_shape=jax.ShapeDtypeStruct(x2d.shape, x2d.dtype),
        compiler_params=pltpu.CompilerParams(
            dimension_semantics=("arbitrary",),
            vmem_limit_bytes=VMEM_LIMIT_BYTES),
        name="ffn",
    )(x2d, pre_g, post_g, wg, wu, wd)


PROJ_TM = 512
ROPE_GROUPS = (1, 3)
Q_GROUPS = (0, 1)


def _rope_slab(y, cos, sin_up, sin_dn):
    up = pltpu.roll(y, ROPE_HALF, 1)
    dn = pltpu.roll(y, LANES - ROPE_HALF, 1)
    return y * cos + up * sin_up + dn * sin_dn


def _in_proj_kernel(x_ref, g_ref, w_ref, wgate_ref, cos_ref, sup_ref, sdn_ref,
                    proj_ref, gate_ref):
    xn = _rms(x_ref[...], g_ref[...]).astype(jnp.bfloat16)
    cos, sup, sdn = cos_ref[...], sup_ref[...], sdn_ref[...]
    for c in range(PROJ_WIDTH // GROUP):
        lo = c * GROUP
        y = jnp.dot(xn, w_ref[:, lo:lo + GROUP],
                    preferred_element_type=jnp.float32)
        for s in range(UNITS):
            slab = y[:, s * LANES:(s + 1) * LANES]
            if c in ROPE_GROUPS:
                slab = _rope_slab(slab, cos, sup, sdn)
            if c in Q_GROUPS:
                slab = slab * (QK_SCALE * LOG2E)
            proj_ref[:, lo + s * LANES:lo + (s + 1) * LANES] = slab.astype(
                proj_ref.dtype)
    gate_ref[...] = jnp.dot(xn, wgate_ref[...],
                            preferred_element_type=jnp.float32)


def _in_proj(x2d, pre_g, w_r, w_gate, cos, sup, sdn, seq):
    t = x2d.shape[0]
    tiles_per_seq = seq // PROJ_TM
    row = lambda w: pl.BlockSpec((PROJ_TM, w), lambda i: (i, 0))
    table = pl.BlockSpec((PROJ_TM, LANES), lambda i: (i % tiles_per_seq, 0))
    return pl.pallas_call(
        _in_proj_kernel,
        grid=(t // PROJ_TM,),
        in_specs=[row(D_MODEL), _const_spec((1, D_MODEL)),
                  _const_spec((D_MODEL, PROJ_WIDTH)),
                  _const_spec((D_MODEL, LANES)), table, table, table],
        out_specs=[row(PROJ_WIDTH), row(LANES)],
        out_shape=[jax.ShapeDtypeStruct((t, PROJ_WIDTH), jnp.bfloat16),
                   jax.ShapeDtypeStruct((t, LANES), jnp.float32)],
        compiler_params=pltpu.CompilerParams(
            dimension_semantics=("arbitrary",),
            vmem_limit_bytes=VMEM_LIMIT_BYTES),
        name="in_proj",
    )(x2d, pre_g, w_r, w_gate, cos, sup, sdn)


def _decay_kernel(gate_ref, b_ref, ccol_ref, crow_ref):
    z = gate_ref[0] + b_ref[...]
    c = jnp.minimum(z, 0.0) - jnp.log1p(jnp.exp(-jnp.abs(z)))
    seq = c.shape[0]
    rows = lax.broadcasted_iota(jnp.int32, c.shape, 0)
    shift = 1
    while shift < seq:
        c = c + jnp.where(rows >= shift, pltpu.roll(c, shift, 0), 0.0)
        shift *= 2
    c = c * LOG2E
    ccol_ref[0] = c
    crow_ref[0] = c.T[:N_FOX_HEADS, :]


def _decay(gate, bias_row, batch, seq):
    gate3 = gate.reshape(batch, seq, LANES)
    return pl.pallas_call(
        _decay_kernel,
        grid=(batch,),
        in_specs=[pl.BlockSpec((1, seq, LANES), lambda b: (b, 0, 0)),
                  _const_spec((1, LANES))],
        out_specs=[pl.BlockSpec((1, seq, LANES), lambda b: (b, 0, 0)),
                   pl.BlockSpec((1, N_FOX_HEADS, seq), lambda b: (b, 0, 0))],
        out_shape=[jax.ShapeDtypeStruct((batch, seq, LANES), jnp.float32),
                   jax.ShapeDtypeStruct((batch, N_FOX_HEADS, seq), jnp.float32)],
        compiler_params=pltpu.CompilerParams(
            dimension_semantics=("arbitrary",),
            vmem_limit_bytes=VMEM_LIMIT_BYTES),
        name="decay",
    )(gate3, bias_row)


ATT_TQ = 256
ATT_TK = 256
ONES_ROWS = 16
VT_ROWS = PAIR + ONES_ROWS


def _attn_body(is_fox, q_ref, k_ref, v_ref, ccol_ref, crow_ref, lam_ref, g_ref,
               o_ref, m_scr, l_scr, acc_scr, qt_scr, s0_scr, s1_scr, vt_scr,
               crep_scr):
    tq, tk = ATT_TQ, ATT_TK
    qi = pl.program_id(1)
    q0 = pl.multiple_of(qi * tq, tq)

    @pl.when(qi == 0)
    def _():
        seq = v_ref.shape[0]
        for u in range(UNITS):
            vt_scr[u, :PAIR, :] = v_ref[:, u * PAIR:(u + 1) * PAIR].T
            vt_scr[u, PAIR:, :] = jnp.ones((ONES_ROWS, seq), vt_scr.dtype)
        if is_fox:
            ccol = ccol_ref[0]
            for h in range(N_FOX_HEADS):
                crep_scr[h] = jnp.broadcast_to(ccol[:, h:h + 1], ccol.shape)

    lane = lax.broadcasted_iota(jnp.int32, (tq, LANES), 1)
    ct = []
    for u in range(UNITS):
        q = q_ref[:, u * PAIR:(u + 1) * PAIR]
        zero = jnp.zeros_like(q)
        q2 = jnp.concatenate([jnp.where(lane < HEAD_DIM, q, zero),
                              jnp.where(lane >= HEAD_DIM, q, zero)], axis=0)
        qt_scr[u] = q2.T
        if is_fox:
            ct.append(jnp.concatenate(
                [crow_ref[0, 2 * u:2 * u + 1, pl.ds(q0, tq)],
                 crow_ref[0, 2 * u + 1:2 * u + 2, pl.ds(q0, tq)]], axis=1))

    m_scr[...] = jnp.full_like(m_scr, MASK_VALUE)
    l_scr[...] = jnp.zeros_like(l_scr)
    acc_scr[...] = jnp.zeros_like(acc_scr)

    def logits_to(slot, j, u):
        k0 = pl.multiple_of(j * tk, tk)
        kb = k_ref[pl.ds(k0, tk), u * PAIR:(u + 1) * PAIR]
        slot[u] = jnp.dot(kb, qt_scr[u], preferred_element_type=jnp.float32)

    def softmax_pv(slot, j, u, masked):
        k0 = pl.multiple_of(j * tk, tk)
        s = slot[u]
        if is_fox:
            rep = lambda h: jnp.tile(crep_scr[h, pl.ds(k0, tk), :],
                                     (1, tq // LANES))
            s = jnp.concatenate([s[:, :tq] - rep(2 * u),
                                 s[:, tq:] - rep(2 * u + 1)], axis=1)
        if masked:
            r = lax.broadcasted_iota(jnp.int32, (tk, 2 * tq), 0)
            c = lax.broadcasted_iota(jnp.int32, (tk, 2 * tq), 1) & (tq - 1)
            s = jnp.where(r <= c, s, MASK_VALUE)
        m_prev = m_scr[u]
        col_max = jnp.max(s, axis=0, keepdims=True)
        if is_fox:
            m_next = jnp.maximum(m_prev, col_max + ct[u])
            offset = m_next - ct[u]
        else:
            m_next = jnp.maximum(m_prev, col_max)
            offset = m_next
        p = jnp.exp2(s - offset).astype(vt_scr.dtype)
        alpha = jnp.exp2(m_prev - m_next)
        vt = vt_scr[u, :, pl.ds(k0, tk)]
        pv = jnp.dot(vt, p, preferred_element_type=jnp.float32)
        l_scr[u] = alpha * l_scr[u] + pv[PAIR:PAIR + 1, :]
        acc_scr[u] = alpha * acc_scr[u] + pv[:PAIR, :]
        m_scr[u] = m_next

    def step(j, cur, nxt, masked):
        ahead = 2
        if nxt is not None:
            for u in range(ahead):
                logits_to(nxt, j + 1, u)
        for u in range(UNITS):
            softmax_pv(cur, j, u, masked)
            if nxt is not None and u + ahead < UNITS:
                logits_to(nxt, j + 1, u + ahead)

    for u in range(UNITS):
        logits_to(s0_scr, 0, u)

    def pair(i, carry):
        step(2 * i, s0_scr, s1_scr, masked=False)
        step(2 * i + 1, s1_scr, s0_scr, masked=False)
        return carry

    lax.fori_loop(0, lax.shift_right_logical(qi, 1), pair, 0)

    @pl.when((qi & 1) == 0)
    def _():
        step(qi, s0_scr, None, masked=True)

    @pl.when((qi & 1) == 1)
    def _():
        step(qi - 1, s0_scr, s1_scr, masked=False)
        step(qi, s1_scr, None, masked=True)

    if not is_fox:
        lam_rows = lam_ref[...]
        dot = lambda i: jnp.sum(lam_rows[i:i + 1] * lam_rows[i + 1:i + 2],
                                axis=1, keepdims=True)
        lam = jnp.exp(dot(0)) - jnp.exp(dot(2)) + LAMBDA_INIT
    for u in range(UNITS):
        o_t = acc_scr[u] / l_scr[u]
        if is_fox:
            out = jnp.concatenate([o_t[:HEAD_DIM, :tq], o_t[HEAD_DIM:, tq:]],
                                  axis=0).T
        else:
            d_t = o_t[:, :tq] - lam * o_t[:, tq:]
            inv = lax.rsqrt(jnp.mean(d_t * d_t, axis=0, keepdims=True) + RMS_EPS)
            out = (d_t * inv).T * g_ref[...] * (1.0 - LAMBDA_INIT)
        o_ref[:, u * PAIR:(u + 1) * PAIR] = out.astype(o_ref.dtype)


def _fox_kernel(q_ref, k_ref, v_ref, ccol_ref, crow_ref, o_ref, *scratch):
    _attn_body(True, q_ref, k_ref, v_ref, ccol_ref, crow_ref, None, None,
               o_ref, *scratch)


def _diff_kernel(q_ref, k_ref, v_ref, lam_ref, g_ref, o_ref, *scratch):
    _attn_body(False, q_ref, k_ref, v_ref, None, None, lam_ref, g_ref,
               o_ref, *scratch, None)


def _attention(proj, batch, seq, group, kernel, extra_specs, extra_args,
               extra_scratch, name):
    nq = seq // ATT_TQ
    q_spec = pl.BlockSpec((ATT_TQ, GROUP), lambda b, i: (b * nq + i, group))
    kv_spec = lambda g: pl.BlockSpec((seq, GROUP), lambda b, i: (b, g))
    out_spec = pl.BlockSpec((ATT_TQ, GROUP), lambda b, i: (b * nq + i, 0))
    stat = pltpu.VMEM((UNITS, 1, 2 * ATT_TQ), jnp.float32)
    acc = pltpu.VMEM((UNITS, PAIR, 2 * ATT_TQ), jnp.float32)
    q_t = pltpu.VMEM((UNITS, PAIR, 2 * ATT_TQ), jnp.bfloat16)
    logit_slot = pltpu.VMEM((UNITS, ATT_TK, 2 * ATT_TQ), jnp.float32)
    v_t = pltpu.VMEM((UNITS, VT_ROWS, seq), jnp.bfloat16)
    return pl.pallas_call(
        kernel,
        grid=(batch, nq),
        in_specs=[q_spec, kv_spec(2 + group), kv_spec(4 + group)] + extra_specs,
        out_specs=out_spec,
        out_shape=jax.ShapeDtypeStruct((batch * seq, GROUP), jnp.bfloat16),
        scratch_shapes=[stat, stat, acc, q_t, logit_slot, logit_slot, v_t]
        + extra_scratch,
        compiler_params=pltpu.CompilerParams(
            dimension_semantics=("arbitrary", "arbitrary"),
            vmem_limit_bytes=VMEM_LIMIT_BYTES),
        name=name,
    )(proj, proj, proj, *extra_args)


OUT_TM = 512


def _out_proj_kernel(x_ref, fox_ref, diff_ref, w_ref, g_ref, o_ref):
    m = jnp.dot(fox_ref[...], w_ref[:FOX_WIDTH, :],
                preferred_element_type=jnp.float32)
    m = m + jnp.dot(diff_ref[...], w_ref[FOX_WIDTH:, :],
                    preferred_element_type=jnp.float32)
    o_ref[...] = x_ref[...] + _rms(m, g_ref[...])


def _out_proj(x2d, fox_o, diff_o, w_out, post_g):
    t = x2d.shape[0]
    row = lambda w: pl.BlockSpec((OUT_TM, w), lambda i: (i, 0))
    return pl.pallas_call(
        _out_proj_kernel,
        grid=(t // OUT_TM,),
        in_specs=[row(D_MODEL), row(FOX_WIDTH), row(DIFF_WIDTH),
                  _const_spec((FOX_WIDTH + DIFF_WIDTH, D_MODEL)),
                  _const_spec((1, D_MODEL))],
        out_specs=row(D_MODEL),
        out_shape=jax.ShapeDtypeStruct(x2d.shape, x2d.dtype),
        compiler_params=pltpu.CompilerParams(
            dimension_semantics=("arbitrary",),
            vmem_limit_bytes=VMEM_LIMIT_BYTES),
        name="out_proj",
    )(x2d, fox_o, diff_o, w_out, post_g)


def _rope_tables(seq):
    inv_freq = ROPE_THETA ** (-jnp.arange(0, ROPE_DIM, 2, dtype=jnp.float32)
                              / ROPE_DIM)
    ang = jnp.arange(seq).astype(jnp.float32)[:, None] * inv_freq[None, :]
    cos, sin = jnp.cos(ang), jnp.sin(ang)
    ones = jnp.ones((seq, HEAD_DIM - ROPE_DIM), jnp.float32)
    zeros = jnp.zeros((seq, HEAD_DIM - ROPE_DIM), jnp.float32)
    z8 = jnp.zeros((seq, ROPE_HALF), jnp.float32)
    cos64 = jnp.concatenate([cos, cos, ones], axis=1)
    up64 = jnp.concatenate([z8, sin, zeros], axis=1)
    dn64 = jnp.concatenate([-sin, z8, zeros], axis=1)
    two = lambda a: jnp.concatenate([a, a], axis=1)
    return two(cos64), two(up64), two(dn64)


def _relayout_w_in(w_in):
    fw, dw = FOX_WIDTH, DIFF_WIDTH
    edges = np.cumsum([0, fw, fw, fw, N_FOX_HEADS, dw, dw, dw])
    fq, fk, fv, fgate, dq, dk, dv = (
        w_in[:, edges[i]:edges[i + 1]] for i in range(7))
    w_r = jnp.concatenate([fq, dq, fk, dk, fv, dv], axis=1)
    w_gate = jnp.pad(fgate, ((0, 0), (0, LANES - N_FOX_HEADS)))
    return w_r.astype(jnp.bfloat16), w_gate.astype(jnp.bfloat16)


def kernel(x, ffn1_pre_g, ffn1_post_g, ffn1_w_gate, ffn1_w_up, ffn1_w_down, mix_pre_g, mix_post_g, w_in, fox_forget_b, diff_lambda_q1, diff_lambda_k1, diff_lambda_q2, diff_lambda_k2, diff_subln_g, w_out, ffn2_pre_g, ffn2_post_g, ffn2_w_gate, ffn2_w_up, ffn2_w_down):
    batch, seq, d = x.shape
    bf = lambda w: w.astype(jnp.bfloat16)
    x2d = x.reshape(batch * seq, d)
    cos, sup, sdn = _rope_tables(seq)
    for l in range(ffn1_pre_g.shape[0]):
        x2d = _ffn(x2d, ffn1_pre_g[l][None], ffn1_post_g[l][None],
                   bf(ffn1_w_gate[l]), bf(ffn1_w_up[l]), bf(ffn1_w_down[l]))

        w_r, w_gate = _relayout_w_in(w_in[l])
        proj, gate = _in_proj(x2d, mix_pre_g[l][None], w_r, w_gate,
                              cos, sup, sdn, seq)
        bias_row = jnp.pad(fox_forget_b[l], (0, LANES - N_FOX_HEADS))[None]
        ccol, crow = _decay(gate, bias_row, batch, seq)

        fox_o = _attention(
            proj, batch, seq, 0, _fox_kernel,
            [pl.BlockSpec((1, seq, LANES), lambda b, i: (b, 0, 0)),
             pl.BlockSpec((1, N_FOX_HEADS, seq), lambda b, i: (b, 0, 0))],
            [ccol, crow],
            [pltpu.VMEM((N_FOX_HEADS, seq, LANES), jnp.float32)], "fox_attn")
        lam_rows = jnp.pad(
            jnp.stack([diff_lambda_q1[l], diff_lambda_k1[l],
                       diff_lambda_q2[l], diff_lambda_k2[l]]),
            ((0, 4), (0, LANES - HEAD_DIM)))
        diff_o = _attention(
            proj, batch, seq, 1, _diff_kernel,
            [_const_spec((8, LANES)), _const_spec((1, PAIR))],
            [lam_rows, diff_subln_g[l][None]], [], "diff_attn")

        x2d = _out_proj(x2d, fox_o, diff_o, bf(w_out[l]), mix_post_g[l][None])

        x2d = _ffn(x2d, ffn2_pre_g[l][None], ffn2_post_g[l][None],
                   bf(ffn2_w_gate[l]), bf(ffn2_w_up[l]), bf(ffn2_w_down[l]))
    return x2d.reshape(batch, seq, d)
```

```python
import math

import jax
import jax.numpy as jnp
import numpy as np
from jax import lax
from jax.experimental import pallas as pl
from jax.experimental.pallas import tpu as pltpu

D_MODEL = 1024
N_FOX_HEADS = 8
N_DIFF_HEADS = 4
HEAD_DIM = 64
PAIR = 2 * HEAD_DIM
FOX_WIDTH = N_FOX_HEADS * HEAD_DIM
DIFF_WIDTH = N_DIFF_HEADS * PAIR
ROPE_THETA = 500000.0
ROPE_DIM = HEAD_DIM // 4
ROPE_HALF = ROPE_DIM // 2
D_FF = 2816
FFN_RESIDUAL_WEIGHT = 0.5
RMS_EPS = 1e-6
LAMBDA_INIT = 0.8 - 0.6 * math.exp(-0.3 * 0)
QK_SCALE = HEAD_DIM ** -0.5
LOG2E = math.log2(math.e)

LANES = 128
MASK_VALUE = -1e30
VMEM_LIMIT_BYTES = 56 * 1024 * 1024

GROUP = FOX_WIDTH
PROJ_WIDTH = 6 * GROUP
UNITS = GROUP // PAIR


def _rms(x, g):
    return x * lax.rsqrt(jnp.mean(x * x, axis=-1, keepdims=True) + RMS_EPS) * g


def _const_spec(shape):
    return pl.BlockSpec(shape, lambda *_: (0,) * len(shape),
                        pipeline_mode=pl.Buffered(1))


FFN_TM = 1024
FFN_SUB = 256
FFN_CHUNKS = ((0, 1536), (1536, 1280))


def _swiglu_half_step(x, pre_g, post_g, wg_ref, wu_ref, wd_ref):
    xn = _rms(x, pre_g).astype(jnp.bfloat16)
    acc = None
    for start, width in FFN_CHUNKS:
        g = jnp.dot(xn, wg_ref[:, start:start + width],
                    preferred_element_type=jnp.float32)
        u = jnp.dot(xn, wu_ref[:, start:start + width],
                    preferred_element_type=jnp.float32)
        h = (g * jax.nn.sigmoid(g) * u).astype(jnp.bfloat16)
        part = jnp.dot(h, wd_ref[start:start + width, :],
                       preferred_element_type=jnp.float32)
        acc = part if acc is None else acc + part
    return x + FFN_RESIDUAL_WEIGHT * _rms(acc, post_g)


def _ffn_kernel(x_ref, pre_g_ref, post_g_ref, wg_ref, wu_ref, wd_ref, o_ref):
    for r in range(0, FFN_TM, FFN_SUB):
        o_ref[r:r + FFN_SUB, :] = _swiglu_half_step(
            x_ref[r:r + FFN_SUB, :], pre_g_ref[...], post_g_ref[...],
            wg_ref, wu_ref, wd_ref)


def _mix_ffn_kernel(x_ref, fox_ref, diff_ref, wout_ref, mix_g_ref,
                    pre_g_ref, post_g_ref, wg_ref, wu_ref, wd_ref, o_ref):
    def mixed(r):
        rows = slice(r, r + FFN_SUB)
        m = jnp.dot(fox_ref[rows, :], wout_ref[:FOX_WIDTH, :],
                    preferred_element_type=jnp.float32)
        m = m + jnp.dot(diff_ref[rows, :], wout_ref[FOX_WIDTH:, :],
                        preferred_element_type=jnp.float32)
        return x_ref[rows, :] + _rms(m, mix_g_ref[...])

    x1_next = mixed(0)
    for r in range(0, FFN_TM, FFN_SUB):
        x1 = x1_next
        if r + FFN_SUB < FFN_TM:
            x1_next = mixed(r + FFN_SUB)
        o_ref[r:r + FFN_SUB, :] = _swiglu_half_step(
            x1, pre_g_ref[...], post_g_ref[...], wg_ref, wu_ref, wd_ref)


def _ffn_weight_specs():
    return [_const_spec((1, D_MODEL)), _const_spec((1, D_MODEL)),
            _const_spec((D_MODEL, D_FF)), _const_spec((D_MODEL, D_FF)),
            _const_spec((D_FF, D_MODEL))]


def _ffn(x2d, pre_g, post_g, wg, wu, wd):
    t = x2d.shape[0]
    row = pl.BlockSpec((FFN_TM, D_MODEL), lambda i: (i, 0))
    return pl.pallas_call(
        _ffn_kernel,
        grid=(t // FFN_TM,),
        in_specs=[row] + _ffn_weight_specs(),
        out_specs=row,
        out_shape=jax.ShapeDtypeStruct(x2d.shape, x2d.dtype),
        compiler_params=pltpu.CompilerParams(
            dimension_semantics=("arbitrary",),
            vmem_limit_bytes=VMEM_LIMIT_BYTES),
        name="ffn",
    )(x2d, pre_g, post_g, wg, wu, wd)


def _mix_ffn(x2d, fox_o, diff_o, w_out, mix_g, pre_g, post_g, wg, wu, wd):
    t = x2d.shape[0]
    row = lambda w: pl.BlockSpec((FFN_TM, w), lambda i: (i, 0))
    return pl.pallas_call(
        _mix_ffn_kernel,
        grid=(t // FFN_TM,),
        in_specs=[row(D_MODEL), row(FOX_WIDTH), row(DIFF_WIDTH),
                  _const_spec((FOX_WIDTH + DIFF_WIDTH, D_MODEL)),
                  _const_spec((1, D_MODEL))] + _ffn_weight_specs(),
        out_specs=row(D_MODEL),
        out_shape=jax.ShapeDtypeStruct(x2d.shape, x2d.dtype),
        compiler_params=pltpu.CompilerParams(
            dimension_semantics=("arbitrary",),
            vmem_limit_bytes=VMEM_LIMIT_BYTES),
        name="mix_ffn",
    )(x2d, fox_o, diff_o, w_out, mix_g, pre_g, post_g, wg, wu, wd)


PROJ_TM = 1024
PROJ_SUB = 256
ROPE_GROUPS = (1, 3)
Q_GROUPS = (0, 1)


def _rope_slab(y, cos, sin_up, sin_dn):
    up = pltpu.roll(y, ROPE_HALF, 1)
    dn = pltpu.roll(y, LANES - ROPE_HALF, 1)
    return y * cos + up * sin_up + dn * sin_dn


def _in_proj_kernel(x_ref, g_ref, w_ref, wgate_ref, cos_ref, sup_ref, sdn_ref,
                    proj_ref, gate_ref):
    for r in range(0, PROJ_TM, PROJ_SUB):
        rows = slice(r, r + PROJ_SUB)
        xn = _rms(x_ref[rows, :], g_ref[...]).astype(jnp.bfloat16)
        cos, sup, sdn = cos_ref[rows, :], sup_ref[rows, :], sdn_ref[rows, :]
        for c in range(PROJ_WIDTH // GROUP):
            lo = c * GROUP
            y = jnp.dot(xn, w_ref[:, lo:lo + GROUP],
                        preferred_element_type=jnp.float32)
            for s in range(UNITS):
                slab = y[:, s * LANES:(s + 1) * LANES]
                if c in ROPE_GROUPS:
                    slab = _rope_slab(slab, cos, sup, sdn)
                if c in Q_GROUPS:
                    slab = slab * (QK_SCALE * LOG2E)
                proj_ref[rows, lo + s * LANES:lo + (s + 1) * LANES] = slab.astype(
                    proj_ref.dtype)
        gate_ref[rows, :] = jnp.dot(xn, wgate_ref[...],
                                    preferred_element_type=jnp.float32)


def _in_proj(x2d, pre_g, w_r, w_gate, cos, sup, sdn, seq):
    t = x2d.shape[0]
    tiles_per_seq = seq // PROJ_TM
    row = lambda w: pl.BlockSpec((PROJ_TM, w), lambda i: (i, 0))
    table = pl.BlockSpec((PROJ_TM, LANES), lambda i: (i % tiles_per_seq, 0))
    return pl.pallas_call(
        _in_proj_kernel,
        grid=(t // PROJ_TM,),
        in_specs=[row(D_MODEL), _const_spec((1, D_MODEL)),
                  _const_spec((D_MODEL, PROJ_WIDTH)),
                  _const_spec((D_MODEL, LANES)), table, table, table],
        out_specs=[row(PROJ_WIDTH), row(LANES)],
        out_shape=[jax.ShapeDtypeStruct((t, PROJ_WIDTH), jnp.bfloat16),
                   jax.ShapeDtypeStruct((t, LANES), jnp.float32)],
        compiler_params=pltpu.CompilerParams(
            dimension_semantics=("arbitrary",),
            vmem_limit_bytes=VMEM_LIMIT_BYTES),
        name="in_proj",
    )(x2d, pre_g, w_r, w_gate, cos, sup, sdn)


def _decay_kernel(gate_ref, b_ref, ccol_ref, crow_ref):
    z = gate_ref[0] + b_ref[...]
    c = jnp.minimum(z, 0.0) - jnp.log1p(jnp.exp(-jnp.abs(z)))
    seq = c.shape[0]
    rows = lax.broadcasted_iota(jnp.int32, c.shape, 0)
    shift = 1
    while shift < seq:
        c = c + jnp.where(rows >= shift, pltpu.roll(c, shift, 0), 0.0)
        shift *= 2
    c = c * LOG2E
    ccol_ref[0] = c
    crow_ref[0] = c.T[:N_FOX_HEADS, :]


def _decay(gate, bias_row, batch, seq):
    gate3 = gate.reshape(batch, seq, LANES)
    return pl.pallas_call(
        _decay_kernel,
        grid=(batch,),
        in_specs=[pl.BlockSpec((1, seq, LANES), lambda b: (b, 0, 0)),
                  _const_spec((1, LANES))],
        out_specs=[pl.BlockSpec((1, seq, LANES), lambda b: (b, 0, 0)),
                   pl.BlockSpec((1, N_FOX_HEADS, seq), lambda b: (b, 0, 0))],
        out_shape=[jax.ShapeDtypeStruct((batch, seq, LANES), jnp.float32),
                   jax.ShapeDtypeStruct((batch, N_FOX_HEADS, seq), jnp.float32)],
        compiler_params=pltpu.CompilerParams(
            dimension_semantics=("arbitrary",),
            vmem_limit_bytes=VMEM_LIMIT_BYTES),
        name="decay",
    )(gate3, bias_row)


ATT_TQ = 256
ATT_TK = 256
ONES_ROWS = 16
VT_ROWS = PAIR + ONES_ROWS


def _attn_body(is_fox, q_ref, k_ref, v_ref, ccol_ref, crow_ref, lam_ref, g_ref,
               o_ref, m_scr, l_scr, acc_scr, qt_scr, s0_scr, s1_scr, vt_scr,
               crep_scr):
    tq, tk = ATT_TQ, ATT_TK
    qi = pl.program_id(1)
    q0 = pl.multiple_of(qi * tq, tq)

    @pl.when(qi == 0)
    def _():
        seq = v_ref.shape[0]
        for u in range(UNITS):
            vt_scr[u, :PAIR, :] = v_ref[:, u * PAIR:(u + 1) * PAIR].T
            vt_scr[u, PAIR:, :] = jnp.ones((ONES_ROWS, seq), vt_scr.dtype)
        if is_fox:
            ccol = ccol_ref[0]
            for h in range(N_FOX_HEADS):
                crep_scr[h] = jnp.broadcast_to(ccol[:, h:h + 1], ccol.shape)

    lane = lax.broadcasted_iota(jnp.int32, (tq, LANES), 1)
    ct = []
    for u in range(UNITS):
        q = q_ref[:, u * PAIR:(u + 1) * PAIR]
        zero = jnp.zeros_like(q)
        q2 = jnp.concatenate([jnp.where(lane < HEAD_DIM, q, zero),
                              jnp.where(lane >= HEAD_DIM, q, zero)], axis=0)
        qt_scr[u] = q2.T
        if is_fox:
            ct.append(jnp.concatenate(
                [crow_ref[0, 2 * u:2 * u + 1, pl.ds(q0, tq)],
                 crow_ref[0, 2 * u + 1:2 * u + 2, pl.ds(q0, tq)]], axis=1))

    m_scr[...] = jnp.full_like(m_scr, MASK_VALUE)
    l_scr[...] = jnp.zeros_like(l_scr)
    acc_scr[...] = jnp.zeros_like(acc_scr)

    def logits_to(slot, j, u):
        k0 = pl.multiple_of(j * tk, tk)
        kb = k_ref[pl.ds(k0, tk), u * PAIR:(u + 1) * PAIR]
        slot[u] = jnp.dot(kb, qt_scr[u], preferred_element_type=jnp.float32)

    def softmax_pv(slot, j, u, masked):
        k0 = pl.multiple_of(j * tk, tk)
        s = slot[u]
        if is_fox:
            rep = lambda h: jnp.tile(crep_scr[h, pl.ds(k0, tk), :],
                                     (1, tq // LANES))
            s = jnp.concatenate([s[:, :tq] - rep(2 * u),
                                 s[:, tq:] - rep(2 * u + 1)], axis=1)
        if masked:
            r = lax.broadcasted_iota(jnp.int32, (tk, 2 * tq), 0)
            c = lax.broadcasted_iota(jnp.int32, (tk, 2 * tq), 1) & (tq - 1)
            s = jnp.where(r <= c, s, MASK_VALUE)
        m_prev = m_scr[u]
        col_max = jnp.max(s, axis=0, keepdims=True)
        if is_fox:
            m_next = jnp.maximum(m_prev, col_max + ct[u])
            offset = m_next - ct[u]
        else:
            m_next = jnp.maximum(m_prev, col_max)
            offset = m_next
        p = jnp.exp2(s - offset).astype(vt_scr.dtype)
        alpha = jnp.exp2(m_prev - m_next)
        vt = vt_scr[u, :, pl.ds(k0, tk)]
        pv = jnp.dot(vt, p, preferred_element_type=jnp.float32)
        l_scr[u] = alpha * l_scr[u] + pv[PAIR:PAIR + 1, :]
        acc_scr[u] = alpha * acc_scr[u] + pv[:PAIR, :]
        m_scr[u] = m_next

    def step(j, cur, nxt, masked):
        ahead = 2
        if nxt is not None:
            for u in range(ahead):
                logits_to(nxt, j + 1, u)
        for u in range(UNITS):
            softmax_pv(cur, j, u, masked)
            if nxt is not None and u + ahead < UNITS:
                logits_to(nxt, j + 1, u + ahead)

    for u in range(UNITS):
        logits_to(s0_scr, 0, u)

    def pair(i, carry):
        step(2 * i, s0_scr, s1_scr, masked=False)
        step(2 * i + 1, s1_scr, s0_scr, masked=False)
        return carry

    lax.fori_loop(0, lax.shift_right_logical(qi, 1), pair, 0)

    @pl.when((qi & 1) == 0)
    def _():
        step(qi, s0_scr, None, masked=True)

    @pl.when((qi & 1) == 1)
    def _():
        step(qi - 1, s0_scr, s1_scr, masked=False)
        step(qi, s1_scr, None, masked=True)

    if not is_fox:
        lam_rows = lam_ref[...]
        dot = lambda i: jnp.sum(lam_rows[i:i + 1] * lam_rows[i + 1:i + 2],
                                axis=1, keepdims=True)
        lam = jnp.exp(dot(0)) - jnp.exp(dot(2)) + LAMBDA_INIT
    for u in range(UNITS):
        o_t = acc_scr[u] / l_scr[u]
        if is_fox:
            out = jnp.concatenate([o_t[:HEAD_DIM, :tq], o_t[HEAD_DIM:, tq:]],
                                  axis=0).T
        else:
            d_t = o_t[:, :tq] - lam * o_t[:, tq:]
            inv = lax.rsqrt(jnp.mean(d_t * d_t, axis=0, keepdims=True) + RMS_EPS)
            out = (d_t * inv).T * g_ref[...] * (1.0 - LAMBDA_INIT)
        o_ref[:, u * PAIR:(u + 1) * PAIR] = out.astype(o_ref.dtype)


def _fox_kernel(q_ref, k_ref, v_ref, ccol_ref, crow_ref, o_ref, *scratch):
    _attn_body(True, q_ref, k_ref, v_ref, ccol_ref, crow_ref, None, None,
               o_ref, *scratch)


def _diff_kernel(q_ref, k_ref, v_ref, lam_ref, g_ref, o_ref, *scratch):
    _attn_body(False, q_ref, k_ref, v_ref, None, None, lam_ref, g_ref,
               o_ref, *scratch, None)


def _attention(proj, batch, seq, group, kernel, extra_specs, extra_args,
               extra_scratch, name):
    nq = seq // ATT_TQ
    q_spec = pl.BlockSpec((ATT_TQ, GROUP), lambda b, i: (b * nq + i, group))
    kv_spec = lambda g: pl.BlockSpec((seq, GROUP), lambda b, i: (b, g))
    out_spec = pl.BlockSpec((ATT_TQ, GROUP), lambda b, i: (b * nq + i, 0))
    stat = pltpu.VMEM((UNITS, 1, 2 * ATT_TQ), jnp.float32)
    acc = pltpu.VMEM((UNITS, PAIR, 2 * ATT_TQ), jnp.float32)
    q_t = pltpu.VMEM((UNITS, PAIR, 2 * ATT_TQ), jnp.bfloat16)
    logit_slot = pltpu.VMEM((UNITS, ATT_TK, 2 * ATT_TQ), jnp.float32)
    v_t = pltpu.VMEM((UNITS, VT_ROWS, seq), jnp.bfloat16)
    return pl.pallas_call(
        kernel,
        grid=(batch, nq),
        in_specs=[q_spec, kv_spec(2 + group), kv_spec(4 + group)] + extra_specs,
        out_specs=out_spec,
        out_shape=jax.ShapeDtypeStruct((batch * seq, GROUP), jnp.bfloat16),
        scratch_shapes=[stat, stat, acc, q_t, logit_slot, logit_slot, v_t]
        + extra_scratch,
        compiler_params=pltpu.CompilerParams(
            dimension_semantics=("arbitrary", "arbitrary"),
            vmem_limit_bytes=VMEM_LIMIT_BYTES),
        name=name,
    )(proj, proj, proj, *extra_args)


def _rope_tables(seq):
    inv_freq = ROPE_THETA ** (-jnp.arange(0, ROPE_DIM, 2, dtype=jnp.float32)
                              / ROPE_DIM)
    ang = jnp.arange(seq).astype(jnp.float32)[:, None] * inv_freq[None, :]
    cos, sin = jnp.cos(ang), jnp.sin(ang)
    ones = jnp.ones((seq, HEAD_DIM - ROPE_DIM), jnp.float32)
    zeros = jnp.zeros((seq, HEAD_DIM - ROPE_DIM), jnp.float32)
    z8 = jnp.zeros((seq, ROPE_HALF), jnp.float32)
    cos64 = jnp.concatenate([cos, cos, ones], axis=1)
    up64 = jnp.concatenate([z8, sin, zeros], axis=1)
    dn64 = jnp.concatenate([-sin, z8, zeros], axis=1)
    two = lambda a: jnp.concatenate([a, a], axis=1)
    return two(cos64), two(up64), two(dn64)


def _relayout_w_in(w_in):
    fw, dw = FOX_WIDTH, DIFF_WIDTH
    edges = np.cumsum([0, fw, fw, fw, N_FOX_HEADS, dw, dw, dw])
    fq, fk, fv, fgate, dq, dk, dv = (
        w_in[:, edges[i]:edges[i + 1]] for i in range(7))
    w_r = jnp.concatenate([fq, dq, fk, dk, fv, dv], axis=1)
    w_gate = jnp.pad(fgate, ((0, 0), (0, LANES - N_FOX_HEADS)))
    return w_r.astype(jnp.bfloat16), w_gate.astype(jnp.bfloat16)


def kernel(x, ffn1_pre_g, ffn1_post_g, ffn1_w_gate, ffn1_w_up, ffn1_w_down, mix_pre_g, mix_post_g, w_in, fox_forget_b, diff_lambda_q1, diff_lambda_k1, diff_lambda_q2, diff_lambda_k2, diff_subln_g, w_out, ffn2_pre_g, ffn2_post_g, ffn2_w_gate, ffn2_w_up, ffn2_w_down):
    batch, seq, d = x.shape
    bf = lambda w: w.astype(jnp.bfloat16)
    x2d = x.reshape(batch * seq, d)
    cos, sup, sdn = _rope_tables(seq)
    for l in range(ffn1_pre_g.shape[0]):
        x2d = _ffn(x2d, ffn1_pre_g[l][None], ffn1_post_g[l][None],
                   bf(ffn1_w_gate[l]), bf(ffn1_w_up[l]), bf(ffn1_w_down[l]))

        w_r, w_gate = _relayout_w_in(w_in[l])
        proj, gate = _in_proj(x2d, mix_pre_g[l][None], w_r, w_gate,
                              cos, sup, sdn, seq)
        bias_row = jnp.pad(fox_forget_b[l], (0, LANES - N_FOX_HEADS))[None]
        ccol, crow = _decay(gate, bias_row, batch, seq)

        fox_o = _attention(
            proj, batch, seq, 0, _fox_kernel,
            [pl.BlockSpec((1, seq, LANES), lambda b, i: (b, 0, 0)),
             pl.BlockSpec((1, N_FOX_HEADS, seq), lambda b, i: (b, 0, 0))],
            [ccol, crow],
            [pltpu.VMEM((N_FOX_HEADS, seq, LANES), jnp.float32)], "fox_attn")
        lam_rows = jnp.pad(
            jnp.stack([diff_lambda_q1[l], diff_lambda_k1[l],
                       diff_lambda_q2[l], diff_lambda_k2[l]]),
            ((0, 4), (0, LANES - HEAD_DIM)))
        diff_o = _attention(
            proj, batch, seq, 1, _diff_kernel,
            [_const_spec((8, LANES)), _const_spec((1, PAIR))],
            [lam_rows, diff_subln_g[l][None]], [], "diff_attn")

        x2d = _mix_ffn(x2d, fox_o, diff_o, bf(w_out[l]), mix_post_g[l][None],
                       ffn2_pre_g[l][None], ffn2_post_g[l][None],
                       bf(ffn2_w_gate[l]), bf(ffn2_w_up[l]), bf(ffn2_w_down[l]))
    return x2d.reshape(batch, seq, d)
```

```python
import math

import jax
import jax.numpy as jnp
import numpy as np
from jax import lax
from jax.experimental import pallas as pl
from jax.experimental.pallas import tpu as pltpu

D_MODEL = 1024
N_FOX_HEADS = 8
N_DIFF_HEADS = 4
HEAD_DIM = 64
PAIR = 2 * HEAD_DIM
FOX_WIDTH = N_FOX_HEADS * HEAD_DIM
DIFF_WIDTH = N_DIFF_HEADS * PAIR
ROPE_THETA = 500000.0
ROPE_DIM = HEAD_DIM // 4
ROPE_HALF = ROPE_DIM // 2
D_FF = 2816
FFN_RESIDUAL_WEIGHT = 0.5
RMS_EPS = 1e-6
LAMBDA_INIT = 0.8 - 0.6 * math.exp(-0.3 * 0)
QK_SCALE = HEAD_DIM ** -0.5
LOG2E = math.log2(math.e)

LANES = 128
MASK_VALUE = -1e30
VMEM_LIMIT_BYTES = 56 * 1024 * 1024

GROUP = FOX_WIDTH
PROJ_WIDTH = 6 * GROUP
UNITS = GROUP // PAIR


def _rms(x, g):
    return x * lax.rsqrt(jnp.mean(x * x, axis=-1, keepdims=True) + RMS_EPS) * g


def _const_spec(shape):
    return pl.BlockSpec(shape, lambda *_: (0,) * len(shape),
                        pipeline_mode=pl.Buffered(1))


FFN_TM = 1024
FFN_SUB = 256
FFN_CHUNKS = ((0, 1536), (1536, 1280))


def _swiglu_half_step(x, pre_g, post_g, wg_ref, wu_ref, wd_ref):
    xn = _rms(x, pre_g).astype(jnp.bfloat16)
    acc = None
    for start, width in FFN_CHUNKS:
        g = jnp.dot(xn, wg_ref[:, start:start + width],
                    preferred_element_type=jnp.float32)
        u = jnp.dot(xn, wu_ref[:, start:start + width],
                    preferred_element_type=jnp.float32)
        h = (g * jax.nn.sigmoid(g) * u).astype(jnp.bfloat16)
        part = jnp.dot(h, wd_ref[start:start + width, :],
                       preferred_element_type=jnp.float32)
        acc = part if acc is None else acc + part
    return x + FFN_RESIDUAL_WEIGHT * _rms(acc, post_g)


def _ffn_kernel(x_ref, pre_g_ref, post_g_ref, wg_ref, wu_ref, wd_ref, o_ref):
    for r in range(0, FFN_TM, FFN_SUB):
        o_ref[r:r + FFN_SUB, :] = _swiglu_half_step(
            x_ref[r:r + FFN_SUB, :], pre_g_ref[...], post_g_ref[...],
            wg_ref, wu_ref, wd_ref)


def _mix_ffn_kernel(x_ref, fox_ref, diff_ref, wout_ref, mix_g_ref,
                    pre_g_ref, post_g_ref, wg_ref, wu_ref, wd_ref, o_ref):
    def mixed(r):
        rows = slice(r, r + FFN_SUB)
        m = jnp.dot(fox_ref[rows, :], wout_ref[:FOX_WIDTH, :],
                    preferred_element_type=jnp.float32)
        m = m + jnp.dot(diff_ref[rows, :], wout_ref[FOX_WIDTH:, :],
                        preferred_element_type=jnp.float32)
        return x_ref[rows, :] + _rms(m, mix_g_ref[...])

    x1_next = mixed(0)
    for r in range(0, FFN_TM, FFN_SUB):
        x1 = x1_next
        if r + FFN_SUB < FFN_TM:
            x1_next = mixed(r + FFN_SUB)
        o_ref[r:r + FFN_SUB, :] = _swiglu_half_step(
            x1, pre_g_ref[...], post_g_ref[...], wg_ref, wu_ref, wd_ref)


def _ffn_weight_specs():
    return [_const_spec((1, D_MODEL)), _const_spec((1, D_MODEL)),
            _const_spec((D_MODEL, D_FF)), _const_spec((D_MODEL, D_FF)),
            _const_spec((D_FF, D_MODEL))]


def _ffn(x2d, pre_g, post_g, wg, wu, wd):
    t = x2d.shape[0]
    row = pl.BlockSpec((FFN_TM, D_MODEL), lambda i: (i, 0))
    return pl.pallas_call(
        _ffn_kernel,
        grid=(t // FFN_TM,),
        in_specs=[row] + _ffn_weight_specs(),
        out_specs=row,
        out_shape=jax.ShapeDtypeStruct(x2d.shape, x2d.dtype),
        compiler_params=pltpu.CompilerParams(
            dimension_semantics=("arbitrary",),
            vmem_limit_bytes=VMEM_LIMIT_BYTES),
        name="ffn",
    )(x2d, pre_g, post_g, wg, wu, wd)


def _mix_ffn(x2d, fox_o, diff_o, w_out, mix_g, pre_g, post_g, wg, wu, wd):
    t = x2d.shape[0]
    row = lambda w: pl.BlockSpec((FFN_TM, w), lambda i: (i, 0))
    return pl.pallas_call(
        _mix_ffn_kernel,
        grid=(t // FFN_TM,),
        in_specs=[row(D_MODEL), row(FOX_WIDTH), row(DIFF_WIDTH),
                  _const_spec((FOX_WIDTH + DIFF_WIDTH, D_MODEL)),
                  _const_spec((1, D_MODEL))] + _ffn_weight_specs(),
        out_specs=row(D_MODEL),
        out_shape=jax.ShapeDtypeStruct(x2d.shape, x2d.dtype),
        compiler_params=pltpu.CompilerParams(
            dimension_semantics=("arbitrary",),
            vmem_limit_bytes=VMEM_LIMIT_BYTES),
        name="mix_ffn",
    )(x2d, fox_o, diff_o, w_out, mix_g, pre_g, post_g, wg, wu, wd)


PROJ_TM = 1024
PROJ_SUB = 256
ROPE_GROUPS = (1, 3)
Q_GROUPS = (0, 1)
FOX_QK_GROUPS = (0, 2)
NSQ_ROWS = 2 * UNITS


def _rope_slab(y, cos, sin_up, sin_dn):
    up = pltpu.roll(y, ROPE_HALF, 1)
    dn = pltpu.roll(y, LANES - ROPE_HALF, 1)
    return y * cos + up * sin_up + dn * sin_dn


def _max_head_sqnorms(slab):
    lane = lax.broadcasted_iota(jnp.int32, slab.shape, 1)
    sq = slab * slab
    head = lambda keep: jnp.max(
        jnp.sum(jnp.where(keep, sq, 0.0), axis=1, keepdims=True),
        axis=0, keepdims=True)
    lane1 = lax.broadcasted_iota(jnp.int32, (1, LANES), 1)
    return jnp.where(lane1 == 0, head(lane < HEAD_DIM),
                     jnp.where(lane1 == 1, head(lane >= HEAD_DIM), 0.0))


def _in_proj_kernel(x_ref, g_ref, w_ref, wgate_ref, cos_ref, sup_ref, sdn_ref,
                    proj_ref, gate_ref, nsq_ref):
    for r in range(0, PROJ_TM, PROJ_SUB):
        rows = slice(r, r + PROJ_SUB)
        xn = _rms(x_ref[rows, :], g_ref[...]).astype(jnp.bfloat16)
        cos, sup, sdn = cos_ref[rows, :], sup_ref[rows, :], sdn_ref[rows, :]
        sqnorms = []
        for c in range(PROJ_WIDTH // GROUP):
            lo = c * GROUP
            y = jnp.dot(xn, w_ref[:, lo:lo + GROUP],
                        preferred_element_type=jnp.float32)
            for s in range(UNITS):
                slab = y[:, s * LANES:(s + 1) * LANES]
                if c in ROPE_GROUPS:
                    slab = _rope_slab(slab, cos, sup, sdn)
                if c in Q_GROUPS:
                    slab = slab * (QK_SCALE * LOG2E)
                slab = slab.astype(proj_ref.dtype)
                proj_ref[rows, lo + s * LANES:lo + (s + 1) * LANES] = slab
                if c in FOX_QK_GROUPS:
                    sqnorms.append(_max_head_sqnorms(slab.astype(jnp.float32)))
        gate_ref[rows, :] = jnp.dot(xn, wgate_ref[...],
                                    preferred_element_type=jnp.float32)
        n0 = (r // PROJ_SUB) * NSQ_ROWS
        nsq_ref[n0:n0 + NSQ_ROWS, :] = jnp.concatenate(sqnorms, axis=0)


def _in_proj(x2d, pre_g, w_r, w_gate, cos, sup, sdn, seq):
    t = x2d.shape[0]
    tiles_per_seq = seq // PROJ_TM
    row = lambda w: pl.BlockSpec((PROJ_TM, w), lambda i: (i, 0))
    table = pl.BlockSpec((PROJ_TM, LANES), lambda i: (i % tiles_per_seq, 0))
    nsq_rows = PROJ_TM // PROJ_SUB * NSQ_ROWS
    return pl.pallas_call(
        _in_proj_kernel,
        grid=(t // PROJ_TM,),
        in_specs=[row(D_MODEL), _const_spec((1, D_MODEL)),
                  _const_spec((D_MODEL, PROJ_WIDTH)),
                  _const_spec((D_MODEL, LANES)), table, table, table],
        out_specs=[row(PROJ_WIDTH), row(LANES),
                   pl.BlockSpec((nsq_rows, LANES), lambda i: (i, 0))],
        out_shape=[jax.ShapeDtypeStruct((t, PROJ_WIDTH), jnp.bfloat16),
                   jax.ShapeDtypeStruct((t, LANES), jnp.float32),
                   jax.ShapeDtypeStruct((t // PROJ_TM * nsq_rows, LANES),
                                        jnp.float32)],
        compiler_params=pltpu.CompilerParams(
            dimension_semantics=("arbitrary",),
            vmem_limit_bytes=VMEM_LIMIT_BYTES),
        name="in_proj",
    )(x2d, pre_g, w_r, w_gate, cos, sup, sdn)


def _decay_kernel(gate_ref, b_ref, ccol_ref, crow_ref):
    z = gate_ref[0] + b_ref[...]
    c = jnp.minimum(z, 0.0) - jnp.log1p(jnp.exp(-jnp.abs(z)))
    seq = c.shape[0]
    rows = lax.broadcasted_iota(jnp.int32, c.shape, 0)
    shift = 1
    while shift < seq:
        c = c + jnp.where(rows >= shift, pltpu.roll(c, shift, 0), 0.0)
        shift *= 2
    c = c * LOG2E
    ccol_ref[0] = c
    crow_ref[0] = c.T[:N_FOX_HEADS, :]


def _decay(gate, bias_row, batch, seq):
    gate3 = gate.reshape(batch, seq, LANES)
    return pl.pallas_call(
        _decay_kernel,
        grid=(batch,),
        in_specs=[pl.BlockSpec((1, seq, LANES), lambda b: (b, 0, 0)),
                  _const_spec((1, LANES))],
        out_specs=[pl.BlockSpec((1, seq, LANES), lambda b: (b, 0, 0)),
                   pl.BlockSpec((1, N_FOX_HEADS, seq), lambda b: (b, 0, 0))],
        out_shape=[jax.ShapeDtypeStruct((batch, seq, LANES), jnp.float32),
                   jax.ShapeDtypeStruct((batch, N_FOX_HEADS, seq), jnp.float32)],
        compiler_params=pltpu.CompilerParams(
            dimension_semantics=("arbitrary",),
            vmem_limit_bytes=VMEM_LIMIT_BYTES),
        name="decay",
    )(gate3, bias_row)


ATT_TQ = 256
ATT_TK = 256
ONES_ROWS = 16
VT_ROWS = PAIR + ONES_ROWS
SKIP_LOG2 = 160.0
BOUND_SLACK = 1.02


def _logit_bound(nsq_ref):
    n = nsq_ref[0:NSQ_ROWS, :]
    for r in range(NSQ_ROWS, nsq_ref.shape[0], NSQ_ROWS):
        n = jnp.maximum(n, nsq_ref[r:r + NSQ_ROWS, :])
    bound = jnp.sqrt(n[:UNITS] * n[UNITS:]) * BOUND_SLACK
    return sum(pltpu.roll(bound[u:u + 1], 2 * u, 1) if u else bound[0:1]
               for u in range(UNITS))


def _first_needed_block(ccol_ref, bh_scr, qi, q0, tk):
    n_blocks = ccol_ref.shape[1] // tk
    c_end = ccol_ref[0, pl.ds(tk - 1, n_blocks, stride=tk), :]
    c_q = ccol_ref[0, pl.ds(q0, 1), :]
    bound = 2.0 * bh_scr[...] + (c_q - c_end)
    block = lax.broadcasted_iota(jnp.int32, bound.shape, 0)
    lane = lax.broadcasted_iota(jnp.int32, bound.shape, 1)
    needed = (lane < N_FOX_HEADS) & jnp.logical_not(bound < -SKIP_LOG2)
    first = jnp.min(jnp.where(needed, block, n_blocks).astype(jnp.float32))
    return jnp.minimum(first.astype(jnp.int32), qi)


def _attn_body(is_fox, q_ref, k_ref, v_ref, ccol_ref, crow_ref, nsq_ref,
               lam_ref, g_ref, o_ref, m_scr, l_scr, acc_scr, qt_scr, s0_scr,
               s1_scr, vt_scr, crep_scr, bh_scr):
    tq, tk = ATT_TQ, ATT_TK
    qi = pl.program_id(1)
    q0 = pl.multiple_of(qi * tq, tq)

    @pl.when(qi == 0)
    def _():
        seq = v_ref.shape[0]
        for u in range(UNITS):
            vt_scr[u, :PAIR, :] = v_ref[:, u * PAIR:(u + 1) * PAIR].T
            vt_scr[u, PAIR:, :] = jnp.ones((ONES_ROWS, seq), vt_scr.dtype)
        if is_fox:
            ccol = ccol_ref[0]
            for h in range(N_FOX_HEADS):
                crep_scr[h] = jnp.broadcast_to(ccol[:, h:h + 1], ccol.shape)
            bh_scr[...] = _logit_bound(nsq_ref)

    j0 = _first_needed_block(ccol_ref, bh_scr, qi, q0, tk) if is_fox else 0
    n_full = qi - j0

    lane = lax.broadcasted_iota(jnp.int32, (tq, LANES), 1)
    ct = []
    for u in range(UNITS):
        q = q_ref[:, u * PAIR:(u + 1) * PAIR]
        zero = jnp.zeros_like(q)
        q2 = jnp.concatenate([jnp.where(lane < HEAD_DIM, q, zero),
                              jnp.where(lane >= HEAD_DIM, q, zero)], axis=0)
        qt_scr[u] = q2.T
        if is_fox:
            ct.append(jnp.concatenate(
                [crow_ref[0, 2 * u:2 * u + 1, pl.ds(q0, tq)],
                 crow_ref[0, 2 * u + 1:2 * u + 2, pl.ds(q0, tq)]], axis=1))

    m_scr[...] = jnp.full_like(m_scr, MASK_VALUE)
    l_scr[...] = jnp.zeros_like(l_scr)
    acc_scr[...] = jnp.zeros_like(acc_scr)

    def logits_to(slot, j, u):
        k0 = pl.multiple_of(j * tk, tk)
        kb = k_ref[pl.ds(k0, tk), u * PAIR:(u + 1) * PAIR]
        slot[u] = jnp.dot(kb, qt_scr[u], preferred_element_type=jnp.float32)

    def softmax_pv(slot, j, u, masked):
        k0 = pl.multiple_of(j * tk, tk)
        s = slot[u]
        if is_fox:
            rep = lambda h: jnp.tile(crep_scr[h, pl.ds(k0, tk), :],
                                     (1, tq // LANES))
            s = jnp.concatenate([s[:, :tq] - rep(2 * u),
                                 s[:, tq:] - rep(2 * u + 1)], axis=1)
        if masked:
            r = lax.broadcasted_iota(jnp.int32, (tk, 2 * tq), 0)
            c = lax.broadcasted_iota(jnp.int32, (tk, 2 * tq), 1) & (tq - 1)
            s = jnp.where(r <= c, s, MASK_VALUE)
        m_prev = m_scr[u]
        col_max = jnp.max(s, axis=0, keepdims=True)
        if is_fox:
            m_next = jnp.maximum(m_prev, col_max + ct[u])
            offset = m_next - ct[u]
        else:
            m_next = jnp.maximum(m_prev, col_max)
            offset = m_next
        p = jnp.exp2(s - offset).astype(vt_scr.dtype)
        alpha = jnp.exp2(m_prev - m_next)
        vt = vt_scr[u, :, pl.ds(k0, tk)]
        pv = jnp.dot(vt, p, preferred_element_type=jnp.float32)
        l_scr[u] = alpha * l_scr[u] + pv[PAIR:PAIR + 1, :]
        acc_scr[u] = alpha * acc_scr[u] + pv[:PAIR, :]
        m_scr[u] = m_next

    def step(j, cur, nxt, masked):
        ahead = 2
        if nxt is not None:
            for u in range(ahead):
                logits_to(nxt, j + 1, u)
        for u in range(UNITS):
            softmax_pv(cur, j, u, masked)
            if nxt is not None and u + ahead < UNITS:
                logits_to(nxt, j + 1, u + ahead)

    for u in range(UNITS):
        logits_to(s0_scr, j0, u)

    def pair(i, carry):
        step(j0 + 2 * i, s0_scr, s1_scr, masked=False)
        step(j0 + 2 * i + 1, s1_scr, s0_scr, masked=False)
        return carry

    lax.fori_loop(0, lax.shift_right_logical(n_full, 1), pair, 0)

    @pl.when((n_full & 1) == 0)
    def _():
        step(qi, s0_scr, None, masked=True)

    @pl.when((n_full & 1) == 1)
    def _():
        step(qi - 1, s0_scr, s1_scr, masked=False)
        step(qi, s1_scr, None, masked=True)

    if not is_fox:
        lam_rows = lam_ref[...]
        dot = lambda i: jnp.sum(lam_rows[i:i + 1] * lam_rows[i + 1:i + 2],
                                axis=1, keepdims=True)
        lam = jnp.exp(dot(0)) - jnp.exp(dot(2)) + LAMBDA_INIT
    for u in range(UNITS):
        o_t = acc_scr[u] / l_scr[u]
        if is_fox:
            out = jnp.concatenate([o_t[:HEAD_DIM, :tq], o_t[HEAD_DIM:, tq:]],
                                  axis=0).T
        else:
            d_t = o_t[:, :tq] - lam * o_t[:, tq:]
            inv = lax.rsqrt(jnp.mean(d_t * d_t, axis=0, keepdims=True) + RMS_EPS)
            out = (d_t * inv).T * g_ref[...] * (1.0 - LAMBDA_INIT)
        o_ref[:, u * PAIR:(u + 1) * PAIR] = out.astype(o_ref.dtype)


def _fox_kernel(q_ref, k_ref, v_ref, ccol_ref, crow_ref, nsq_ref, o_ref,
                *scratch):
    _attn_body(True, q_ref, k_ref, v_ref, ccol_ref, crow_ref, nsq_ref, None,
               None, o_ref, *scratch)


def _diff_kernel(q_ref, k_ref, v_ref, lam_ref, g_ref, o_ref, *scratch):
    _attn_body(False, q_ref, k_ref, v_ref, None, None, None, lam_ref, g_ref,
               o_ref, *scratch, None, None)


def _attention(proj, batch, seq, group, kernel, extra_specs, extra_args,
               extra_scratch, name):
    nq = seq // ATT_TQ
    q_spec = pl.BlockSpec((ATT_TQ, GROUP), lambda b, i: (b * nq + i, group))
    kv_spec = lambda g: pl.BlockSpec((seq, GROUP), lambda b, i: (b, g))
    out_spec = pl.BlockSpec((ATT_TQ, GROUP), lambda b, i: (b * nq + i, 0))
    stat = pltpu.VMEM((UNITS, 1, 2 * ATT_TQ), jnp.float32)
    acc = pltpu.VMEM((UNITS, PAIR, 2 * ATT_TQ), jnp.float32)
    q_t = pltpu.VMEM((UNITS, PAIR, 2 * ATT_TQ), jnp.bfloat16)
    logit_slot = pltpu.VMEM((UNITS, ATT_TK, 2 * ATT_TQ), jnp.float32)
    v_t = pltpu.VMEM((UNITS, VT_ROWS, seq), jnp.bfloat16)
    return pl.pallas_call(
        kernel,
        grid=(batch, nq),
        in_specs=[q_spec, kv_spec(2 + group), kv_spec(4 + group)] + extra_specs,
        out_specs=out_spec,
        out_shape=jax.ShapeDtypeStruct((batch * seq, GROUP), jnp.bfloat16),
        scratch_shapes=[stat, stat, acc, q_t, logit_slot, logit_slot, v_t]
        + extra_scratch,
        compiler_params=pltpu.CompilerParams(
            dimension_semantics=("arbitrary", "arbitrary"),
            vmem_limit_bytes=VMEM_LIMIT_BYTES),
        name=name,
    )(proj, proj, proj, *extra_args)


def _rope_tables(seq):
    inv_freq = ROPE_THETA ** (-jnp.arange(0, ROPE_DIM, 2, dtype=jnp.float32)
                              / ROPE_DIM)
    ang = jnp.arange(seq).astype(jnp.float32)[:, None] * inv_freq[None, :]
    cos, sin = jnp.cos(ang), jnp.sin(ang)
    ones = jnp.ones((seq, HEAD_DIM - ROPE_DIM), jnp.float32)
    zeros = jnp.zeros((seq, HEAD_DIM - ROPE_DIM), jnp.float32)
    z8 = jnp.zeros((seq, ROPE_HALF), jnp.float32)
    cos64 = jnp.concatenate([cos, cos, ones], axis=1)
    up64 = jnp.concatenate([z8, sin, zeros], axis=1)
    dn64 = jnp.concatenate([-sin, z8, zeros], axis=1)
    two = lambda a: jnp.concatenate([a, a], axis=1)
    return two(cos64), two(up64), two(dn64)


def _relayout_w_in(w_in):
    fw, dw = FOX_WIDTH, DIFF_WIDTH
    edges = np.cumsum([0, fw, fw, fw, N_FOX_HEADS, dw, dw, dw])
    fq, fk, fv, fgate, dq, dk, dv = (
        w_in[:, edges[i]:edges[i + 1]] for i in range(7))
    w_r = jnp.concatenate([fq, dq, fk, dk, fv, dv], axis=1)
    w_gate = jnp.pad(fgate, ((0, 0), (0, LANES - N_FOX_HEADS)))
    return w_r.astype(jnp.bfloat16), w_gate.astype(jnp.bfloat16)


def kernel(x, ffn1_pre_g, ffn1_post_g, ffn1_w_gate, ffn1_w_up, ffn1_w_down, mix_pre_g, mix_post_g, w_in, fox_forget_b, diff_lambda_q1, diff_lambda_k1, diff_lambda_q2, diff_lambda_k2, diff_subln_g, w_out, ffn2_pre_g, ffn2_post_g, ffn2_w_gate, ffn2_w_up, ffn2_w_down):
    batch, seq, d = x.shape
    bf = lambda w: w.astype(jnp.bfloat16)
    x2d = x.reshape(batch * seq, d)
    cos, sup, sdn = _rope_tables(seq)
    for l in range(ffn1_pre_g.shape[0]):
        x2d = _ffn(x2d, ffn1_pre_g[l][None], ffn1_post_g[l][None],
                   bf(ffn1_w_gate[l]), bf(ffn1_w_up[l]), bf(ffn1_w_down[l]))

        w_r, w_gate = _relayout_w_in(w_in[l])
        proj, gate, nsq = _in_proj(x2d, mix_pre_g[l][None], w_r, w_gate,
                                   cos, sup, sdn, seq)
        bias_row = jnp.pad(fox_forget_b[l], (0, LANES - N_FOX_HEADS))[None]
        ccol, crow = _decay(gate, bias_row, batch, seq)

        fox_o = _attention(
            proj, batch, seq, 0, _fox_kernel,
            [pl.BlockSpec((1, seq, LANES), lambda b, i: (b, 0, 0)),
             pl.BlockSpec((1, N_FOX_HEADS, seq), lambda b, i: (b, 0, 0)),
             pl.BlockSpec((seq // PROJ_SUB * NSQ_ROWS, LANES),
                          lambda b, i: (b, 0))],
            [ccol, crow, nsq],
            [pltpu.VMEM((N_FOX_HEADS, seq, LANES), jnp.float32),
             pltpu.VMEM((1, LANES), jnp.float32)], "fox_attn")
        lam_rows = jnp.pad(
            jnp.stack([diff_lambda_q1[l], diff_lambda_k1[l],
                       diff_lambda_q2[l], diff_lambda_k2[l]]),
            ((0, 4), (0, LANES - HEAD_DIM)))
        diff_o = _attention(
            proj, batch, seq, 1, _diff_kernel,
            [_const_spec((8, LANES)), _const_spec((1, PAIR))],
            [lam_rows, diff_subln_g[l][None]], [], "diff_attn")

        x2d = _mix_ffn(x2d, fox_o, diff_o, bf(w_out[l]), mix_post_g[l][None],
                       ffn2_pre_g[l][None], ffn2_post_g[l][None],
                       bf(ffn2_w_gate[l]), bf(ffn2_w_up[l]), bf(ffn2_w_down[l]))
    return x2d.reshape(batch, seq, d)
```

```python
import math

import jax
import jax.numpy as jnp
import numpy as np
from jax import lax
from jax.experimental import pallas as pl
from jax.experimental.pallas import tpu as pltpu

D_MODEL = 1024
N_FOX_HEADS = 8
N_DIFF_HEADS = 4
HEAD_DIM = 64
PAIR = 2 * HEAD_DIM
FOX_WIDTH = N_FOX_HEADS * HEAD_DIM
DIFF_WIDTH = N_DIFF_HEADS * PAIR
ROPE_THETA = 500000.0
ROPE_DIM = HEAD_DIM // 4
ROPE_HALF = ROPE_DIM // 2
D_FF = 2816
FFN_RESIDUAL_WEIGHT = 0.5
RMS_EPS = 1e-6
LAMBDA_INIT = 0.8 - 0.6 * math.exp(-0.3 * 0)
QK_SCALE = HEAD_DIM ** -0.5
LOG2E = math.log2(math.e)

LANES = 128
MASK_VALUE = -1e30
VMEM_LIMIT_BYTES = 56 * 1024 * 1024

GROUP = FOX_WIDTH
PROJ_WIDTH = 6 * GROUP
UNITS = GROUP // PAIR


def _rms(x, g):
    return x * lax.rsqrt(jnp.mean(x * x, axis=-1, keepdims=True) + RMS_EPS) * g


def _const_spec(shape):
    return pl.BlockSpec(shape, lambda *_: (0,) * len(shape),
                        pipeline_mode=pl.Buffered(1))


FFN_TM = 1024
FFN_SUB = 256
FFN_CHUNKS = ((0, 1536), (1536, 1280))


def _swiglu_half_step(x, pre_g, post_g, wg_ref, wu_ref, wd_ref):
    xn = _rms(x, pre_g).astype(jnp.bfloat16)
    acc = None
    for start, width in FFN_CHUNKS:
        g = jnp.dot(xn, wg_ref[:, start:start + width],
                    preferred_element_type=jnp.float32)
        u = jnp.dot(xn, wu_ref[:, start:start + width],
                    preferred_element_type=jnp.float32)
        h = (g * jax.nn.sigmoid(g) * u).astype(jnp.bfloat16)
        part = jnp.dot(h, wd_ref[start:start + width, :],
                       preferred_element_type=jnp.float32)
        acc = part if acc is None else acc + part
    return x + FFN_RESIDUAL_WEIGHT * _rms(acc, post_g)


def _ffn_kernel(x_ref, pre_g_ref, post_g_ref, wg_ref, wu_ref, wd_ref, o_ref):
    for r in range(0, FFN_TM, FFN_SUB):
        o_ref[r:r + FFN_SUB, :] = _swiglu_half_step(
            x_ref[r:r + FFN_SUB, :], pre_g_ref[...], post_g_ref[...],
            wg_ref, wu_ref, wd_ref)


def _mix_ffn_kernel(x_ref, fox_ref, diff_ref, wout_ref, mix_g_ref,
                    pre_g_ref, post_g_ref, wg_ref, wu_ref, wd_ref, o_ref):
    def mixed(r):
        rows = slice(r, r + FFN_SUB)
        m = jnp.dot(fox_ref[rows, :], wout_ref[:FOX_WIDTH, :],
                    preferred_element_type=jnp.float32)
        m = m + jnp.dot(diff_ref[rows, :], wout_ref[FOX_WIDTH:, :],
                        preferred_element_type=jnp.float32)
        return x_ref[rows, :] + _rms(m, mix_g_ref[...])

    x1_next = mixed(0)
    for r in range(0, FFN_TM, FFN_SUB):
        x1 = x1_next
        if r + FFN_SUB < FFN_TM:
            x1_next = mixed(r + FFN_SUB)
        o_ref[r:r + FFN_SUB, :] = _swiglu_half_step(
            x1, pre_g_ref[...], post_g_ref[...], wg_ref, wu_ref, wd_ref)


def _ffn_weight_specs():
    return [_const_spec((1, D_MODEL)), _const_spec((1, D_MODEL)),
            _const_spec((D_MODEL, D_FF)), _const_spec((D_MODEL, D_FF)),
            _const_spec((D_FF, D_MODEL))]


def _ffn(x2d, pre_g, post_g, wg, wu, wd):
    t = x2d.shape[0]
    row = pl.BlockSpec((FFN_TM, D_MODEL), lambda i: (i, 0))
    return pl.pallas_call(
        _ffn_kernel,
        grid=(t // FFN_TM,),
        in_specs=[row] + _ffn_weight_specs(),
        out_specs=row,
        out_shape=jax.ShapeDtypeStruct(x2d.shape, x2d.dtype),
        compiler_params=pltpu.CompilerParams(
            dimension_semantics=("arbitrary",),
            vmem_limit_bytes=VMEM_LIMIT_BYTES),
        name="ffn",
    )(x2d, pre_g, post_g, wg, wu, wd)


def _mix_ffn(x2d, fox_o, diff_o, w_out, mix_g, pre_g, post_g, wg, wu, wd):
    t = x2d.shape[0]
    row = lambda w: pl.BlockSpec((FFN_TM, w), lambda i: (i, 0))
    return pl.pallas_call(
        _mix_ffn_kernel,
        grid=(t // FFN_TM,),
        in_specs=[row(D_MODEL), row(FOX_WIDTH), row(DIFF_WIDTH),
                  _const_spec((FOX_WIDTH + DIFF_WIDTH, D_MODEL)),
                  _const_spec((1, D_MODEL))] + _ffn_weight_specs(),
        out_specs=row(D_MODEL),
        out_shape=jax.ShapeDtypeStruct(x2d.shape, x2d.dtype),
        compiler_params=pltpu.CompilerParams(
            dimension_semantics=("arbitrary",),
            vmem_limit_bytes=VMEM_LIMIT_BYTES),
        name="mix_ffn",
    )(x2d, fox_o, diff_o, w_out, mix_g, pre_g, post_g, wg, wu, wd)


PROJ_TM = 1024
PROJ_SUB = 256
ROPE_GROUPS = (1, 3)
Q_GROUPS = (0, 1)
FOX_QK_GROUPS = (0, 2)
NSQ_ROWS = 2 * UNITS


def _rope_slab(y, cos, sin_up, sin_dn):
    up = pltpu.roll(y, ROPE_HALF, 1)
    dn = pltpu.roll(y, LANES - ROPE_HALF, 1)
    return y * cos + up * sin_up + dn * sin_dn


def _max_head_sqnorms(slab):
    lane = lax.broadcasted_iota(jnp.int32, slab.shape, 1)
    sq = slab * slab
    head = lambda keep: jnp.max(
        jnp.sum(jnp.where(keep, sq, 0.0), axis=1, keepdims=True),
        axis=0, keepdims=True)
    lane1 = lax.broadcasted_iota(jnp.int32, (1, LANES), 1)
    return jnp.where(lane1 == 0, head(lane < HEAD_DIM),
                     jnp.where(lane1 == 1, head(lane >= HEAD_DIM), 0.0))


def _in_proj_kernel(x_ref, g_ref, w_ref, wgate_ref, cos_ref, sup_ref, sdn_ref,
                    proj_ref, gate_ref, nsq_ref):
    for r in range(0, PROJ_TM, PROJ_SUB):
        rows = slice(r, r + PROJ_SUB)
        xn = _rms(x_ref[rows, :], g_ref[...]).astype(jnp.bfloat16)
        cos, sup, sdn = cos_ref[rows, :], sup_ref[rows, :], sdn_ref[rows, :]
        sqnorms = []
        for c in range(PROJ_WIDTH // GROUP):
            lo = c * GROUP
            y = jnp.dot(xn, w_ref[:, lo:lo + GROUP],
                        preferred_element_type=jnp.float32)
            for s in range(UNITS):
                slab = y[:, s * LANES:(s + 1) * LANES]
                if c in ROPE_GROUPS:
                    slab = _rope_slab(slab, cos, sup, sdn)
                if c in Q_GROUPS:
                    slab = slab * (QK_SCALE * LOG2E)
                slab = slab.astype(proj_ref.dtype)
                proj_ref[rows, lo + s * LANES:lo + (s + 1) * LANES] = slab
                if c in FOX_QK_GROUPS:
                    sqnorms.append(_max_head_sqnorms(slab.astype(jnp.float32)))
        gate_ref[rows, :] = jnp.dot(xn, wgate_ref[...],
                                    preferred_element_type=jnp.float32)
        n0 = (r // PROJ_SUB) * NSQ_ROWS
        nsq_ref[n0:n0 + NSQ_ROWS, :] = jnp.concatenate(sqnorms, axis=0)


def _in_proj(x2d, pre_g, w_r, w_gate, cos, sup, sdn, seq):
    t = x2d.shape[0]
    tiles_per_seq = seq // PROJ_TM
    row = lambda w: pl.BlockSpec((PROJ_TM, w), lambda i: (i, 0))
    table = pl.BlockSpec((PROJ_TM, LANES), lambda i: (i % tiles_per_seq, 0))
    nsq_rows = PROJ_TM // PROJ_SUB * NSQ_ROWS
    return pl.pallas_call(
        _in_proj_kernel,
        grid=(t // PROJ_TM,),
        in_specs=[row(D_MODEL), _const_spec((1, D_MODEL)),
                  _const_spec((D_MODEL, PROJ_WIDTH)),
                  _const_spec((D_MODEL, LANES)), table, table, table],
        out_specs=[row(PROJ_WIDTH), row(LANES),
                   pl.BlockSpec((nsq_rows, LANES), lambda i: (i, 0))],
        out_shape=[jax.ShapeDtypeStruct((t, PROJ_WIDTH), jnp.bfloat16),
                   jax.ShapeDtypeStruct((t, LANES), jnp.float32),
                   jax.ShapeDtypeStruct((t // PROJ_TM * nsq_rows, LANES),
                                        jnp.float32)],
        compiler_params=pltpu.CompilerParams(
            dimension_semantics=("arbitrary",),
            vmem_limit_bytes=VMEM_LIMIT_BYTES),
        name="in_proj",
    )(x2d, pre_g, w_r, w_gate, cos, sup, sdn)


def _decay_kernel(gate_ref, b_ref, ccol_ref, crow_ref):
    z = gate_ref[0] + b_ref[...]
    c = jnp.minimum(z, 0.0) - jnp.log1p(jnp.exp(-jnp.abs(z)))
    seq = c.shape[0]
    rows = lax.broadcasted_iota(jnp.int32, c.shape, 0)
    shift = 1
    while shift < seq:
        c = c + jnp.where(rows >= shift, pltpu.roll(c, shift, 0), 0.0)
        shift *= 2
    c = c * LOG2E
    ccol_ref[0] = c
    crow_ref[0] = c.T[:N_FOX_HEADS, :]


def _decay(gate, bias_row, batch, seq):
    gate3 = gate.reshape(batch, seq, LANES)
    return pl.pallas_call(
        _decay_kernel,
        grid=(batch,),
        in_specs=[pl.BlockSpec((1, seq, LANES), lambda b: (b, 0, 0)),
                  _const_spec((1, LANES))],
        out_specs=[pl.BlockSpec((1, seq, LANES), lambda b: (b, 0, 0)),
                   pl.BlockSpec((1, N_FOX_HEADS, seq), lambda b: (b, 0, 0))],
        out_shape=[jax.ShapeDtypeStruct((batch, seq, LANES), jnp.float32),
                   jax.ShapeDtypeStruct((batch, N_FOX_HEADS, seq), jnp.float32)],
        compiler_params=pltpu.CompilerParams(
            dimension_semantics=("arbitrary",),
            vmem_limit_bytes=VMEM_LIMIT_BYTES),
        name="decay",
    )(gate3, bias_row)


ATT_TQ = 512
ATT_HALF = ATT_TQ // 2
ATT_TK = ATT_HALF
ONES_ROWS = 16
VT_ROWS = PAIR + ONES_ROWS
SKIP_LOG2 = 160.0
BOUND_SLACK = 1.02


def _logit_bound(nsq_ref):
    n = nsq_ref[0:NSQ_ROWS, :]
    for r in range(NSQ_ROWS, nsq_ref.shape[0], NSQ_ROWS):
        n = jnp.maximum(n, nsq_ref[r:r + NSQ_ROWS, :])
    bound = jnp.sqrt(n[:UNITS] * n[UNITS:]) * BOUND_SLACK
    return sum(pltpu.roll(bound[u:u + 1], 2 * u, 1) if u else bound[0:1]
               for u in range(UNITS))


def _first_needed_block(ccol_ref, bh_scr, q0, tk, limit):
    n_blocks = ccol_ref.shape[1] // tk
    c_end = ccol_ref[0, pl.ds(tk - 1, n_blocks, stride=tk), :]
    c_q = ccol_ref[0, pl.ds(q0, 1), :]
    bound = 2.0 * bh_scr[...] + (c_q - c_end)
    block = lax.broadcasted_iota(jnp.int32, bound.shape, 0)
    lane = lax.broadcasted_iota(jnp.int32, bound.shape, 1)
    needed = (lane < N_FOX_HEADS) & jnp.logical_not(bound < -SKIP_LOG2)
    first = jnp.min(jnp.where(needed, block, n_blocks).astype(jnp.float32))
    return jnp.minimum(first.astype(jnp.int32), limit)


def _attn_body(is_fox, q_ref, k_ref, v_ref, ccol_ref, crow_ref, nsq_ref,
               lam_ref, g_ref, o_ref, m_scr, l_scr, acc_scr, qt_scr, s0_scr,
               s1_scr, vt_scr, crep_scr, bh_scr):
    tq, th, tk = ATT_TQ, ATT_HALF, ATT_TK
    qi = pl.program_id(1)
    q0 = pl.multiple_of(qi * tq, tq)

    @pl.when(qi == 0)
    def _():
        seq = v_ref.shape[0]
        for u in range(UNITS):
            vt_scr[u, :PAIR, :] = v_ref[:, u * PAIR:(u + 1) * PAIR].T
            vt_scr[u, PAIR:, :] = jnp.ones((ONES_ROWS, seq), vt_scr.dtype)
        if is_fox:
            ccol = ccol_ref[0]
            for h in range(N_FOX_HEADS):
                crep_scr[h] = jnp.broadcast_to(ccol[:, h:h + 1], ccol.shape)
            bh_scr[...] = _logit_bound(nsq_ref)

    first_diag = 2 * qi
    j0 = (_first_needed_block(ccol_ref, bh_scr, q0, tk, first_diag)
          if is_fox else 0)
    n_full = first_diag - j0

    lane = lax.broadcasted_iota(jnp.int32, (th, LANES), 1)
    ct = []
    for u in range(UNITS):
        pieces, ct_u = [], []
        for h in range(2):
            q = q_ref[h * th:(h + 1) * th, u * PAIR:(u + 1) * PAIR]
            zero = jnp.zeros_like(q)
            pieces += [jnp.where(lane < HEAD_DIM, q, zero),
                       jnp.where(lane >= HEAD_DIM, q, zero)]
            if is_fox:
                t0 = pl.multiple_of(q0 + h * th, th)
                ct_u += [crow_ref[0, 2 * u:2 * u + 1, pl.ds(t0, th)],
                         crow_ref[0, 2 * u + 1:2 * u + 2, pl.ds(t0, th)]]
        qt_scr[u] = jnp.concatenate(pieces, axis=0).T
        if is_fox:
            ct.append(jnp.concatenate(ct_u, axis=1))

    m_scr[...] = jnp.full_like(m_scr, MASK_VALUE)
    l_scr[...] = jnp.zeros_like(l_scr)
    acc_scr[...] = jnp.zeros_like(acc_scr)

    def logits_to(slot, j, u, lo=0):
        k0 = pl.multiple_of(j * tk, tk)
        kb = k_ref[pl.ds(k0, tk), u * PAIR:(u + 1) * PAIR]
        slot[u, :, lo:] = jnp.dot(kb, qt_scr[u, :, lo:],
                                  preferred_element_type=jnp.float32)

    def softmax_pv(slot, j, u, lo, masked):
        k0 = pl.multiple_of(j * tk, tk)
        s = slot[u, :, lo:]
        width = s.shape[1]
        if is_fox:
            rep = lambda h: jnp.tile(crep_scr[h, pl.ds(k0, tk), :],
                                     (1, th // LANES))
            s = jnp.concatenate(
                [s[:, m * th:(m + 1) * th] - rep(2 * u + (m & 1))
                 for m in range(width // th)], axis=1)
        if masked:
            r = lax.broadcasted_iota(jnp.int32, (tk, width), 0)
            c = lax.broadcasted_iota(jnp.int32, (tk, width), 1)
            keep = r <= (c & (th - 1))
            if lo == 0:
                keep = keep | (c >= tq)
            s = jnp.where(keep, s, MASK_VALUE)
        m_prev = m_scr[u, :, lo:]
        col_max = jnp.max(s, axis=0, keepdims=True)
        if is_fox:
            m_next = jnp.maximum(m_prev, col_max + ct[u][:, lo:])
            offset = m_next - ct[u][:, lo:]
        else:
            m_next = jnp.maximum(m_prev, col_max)
            offset = m_next
        p = jnp.exp2(s - offset).astype(vt_scr.dtype)
        alpha = jnp.exp2(m_prev - m_next)
        vt = vt_scr[u, :, pl.ds(k0, tk)]
        pv = jnp.dot(vt, p, preferred_element_type=jnp.float32)
        l_scr[u, :, lo:] = alpha * l_scr[u, :, lo:] + pv[PAIR:PAIR + 1, :]
        acc_scr[u, :, lo:] = alpha * acc_scr[u, :, lo:] + pv[:PAIR, :]
        m_scr[u, :, lo:] = m_next

    def step(j, cur, nxt, lo=0, masked=False, nxt_lo=0):
        ahead = 1
        if nxt is not None:
            for u in range(ahead):
                logits_to(nxt, j + 1, u, nxt_lo)
        for u in range(UNITS):
            softmax_pv(cur, j, u, lo, masked)
            if nxt is not None and u + ahead < UNITS:
                logits_to(nxt, j + 1, u + ahead, nxt_lo)

    for u in range(UNITS):
        logits_to(s0_scr, j0, u)

    def pair(i, carry):
        step(j0 + 2 * i, s0_scr, s1_scr)
        step(j0 + 2 * i + 1, s1_scr, s0_scr)
        return carry

    lax.fori_loop(0, lax.shift_right_logical(n_full, 1), pair, 0)

    def diagonal(cur, other):
        step(first_diag, cur, other, masked=True, nxt_lo=tq)
        step(first_diag + 1, other, None, lo=tq, masked=True)

    if is_fox:
        @pl.when((n_full & 1) == 0)
        def _():
            diagonal(s0_scr, s1_scr)

        @pl.when((n_full & 1) == 1)
        def _():
            step(first_diag - 1, s0_scr, s1_scr)
            diagonal(s1_scr, s0_scr)
    else:
        diagonal(s0_scr, s1_scr)

    if not is_fox:
        lam_rows = lam_ref[...]
        dot = lambda i: jnp.sum(lam_rows[i:i + 1] * lam_rows[i + 1:i + 2],
                                axis=1, keepdims=True)
        lam = jnp.exp(dot(0)) - jnp.exp(dot(2)) + LAMBDA_INIT
    for u in range(UNITS):
        o_t = acc_scr[u] / l_scr[u]
        for h in range(2):
            a = o_t[:, 2 * h * th:(2 * h + 1) * th]
            b = o_t[:, (2 * h + 1) * th:(2 * h + 2) * th]
            if is_fox:
                out = jnp.concatenate([a[:HEAD_DIM], b[HEAD_DIM:]], axis=0).T
            else:
                d_t = a - lam * b
                inv = lax.rsqrt(jnp.mean(d_t * d_t, axis=0, keepdims=True)
                                + RMS_EPS)
                out = (d_t * inv).T * g_ref[...] * (1.0 - LAMBDA_INIT)
            o_ref[h * th:(h + 1) * th, u * PAIR:(u + 1) * PAIR] = out.astype(
                o_ref.dtype)


def _fox_kernel(q_ref, k_ref, v_ref, ccol_ref, crow_ref, nsq_ref, o_ref,
                *scratch):
    _attn_body(True, q_ref, k_ref, v_ref, ccol_ref, crow_ref, nsq_ref, None,
               None, o_ref, *scratch)


def _diff_kernel(q_ref, k_ref, v_ref, lam_ref, g_ref, o_ref, *scratch):
    _attn_body(False, q_ref, k_ref, v_ref, None, None, None, lam_ref, g_ref,
               o_ref, *scratch, None, None)


def _attention(proj, batch, seq, group, kernel, extra_specs, extra_args,
               extra_scratch, name):
    nq = seq // ATT_TQ
    q_spec = pl.BlockSpec((ATT_TQ, GROUP), lambda b, i: (b * nq + i, group))
    kv_spec = lambda g: pl.BlockSpec((seq, GROUP), lambda b, i: (b, g))
    out_spec = pl.BlockSpec((ATT_TQ, GROUP), lambda b, i: (b * nq + i, 0))
    stat = pltpu.VMEM((UNITS, 1, 2 * ATT_TQ), jnp.float32)
    acc = pltpu.VMEM((UNITS, PAIR, 2 * ATT_TQ), jnp.float32)
    q_t = pltpu.VMEM((UNITS, PAIR, 2 * ATT_TQ), jnp.bfloat16)
    logit_slot = pltpu.VMEM((UNITS, ATT_TK, 2 * ATT_TQ), jnp.float32)
    v_t = pltpu.VMEM((UNITS, VT_ROWS, seq), jnp.bfloat16)
    return pl.pallas_call(
        kernel,
        grid=(batch, nq),
        in_specs=[q_spec, kv_spec(2 + group), kv_spec(4 + group)] + extra_specs,
        out_specs=out_spec,
        out_shape=jax.ShapeDtypeStruct((batch * seq, GROUP), jnp.bfloat16),
        scratch_shapes=[stat, stat, acc, q_t, logit_slot, logit_slot, v_t]
        + extra_scratch,
        compiler_params=pltpu.CompilerParams(
            dimension_semantics=("arbitrary", "arbitrary"),
            vmem_limit_bytes=VMEM_LIMIT_BYTES),
        name=name,
    )(proj, proj, proj, *extra_args)


def _rope_tables(seq):
    inv_freq = ROPE_THETA ** (-jnp.arange(0, ROPE_DIM, 2, dtype=jnp.float32)
                              / ROPE_DIM)
    ang = jnp.arange(seq).astype(jnp.float32)[:, None] * inv_freq[None, :]
    cos, sin = jnp.cos(ang), jnp.sin(ang)
    ones = jnp.ones((seq, HEAD_DIM - ROPE_DIM), jnp.float32)
    zeros = jnp.zeros((seq, HEAD_DIM - ROPE_DIM), jnp.float32)
    z8 = jnp.zeros((seq, ROPE_HALF), jnp.float32)
    cos64 = jnp.concatenate([cos, cos, ones], axis=1)
    up64 = jnp.concatenate([z8, sin, zeros], axis=1)
    dn64 = jnp.concatenate([-sin, z8, zeros], axis=1)
    two = lambda a: jnp.concatenate([a, a], axis=1)
    return two(cos64), two(up64), two(dn64)


def _relayout_w_in(w_in):
    fw, dw = FOX_WIDTH, DIFF_WIDTH
    edges = np.cumsum([0, fw, fw, fw, N_FOX_HEADS, dw, dw, dw])
    fq, fk, fv, fgate, dq, dk, dv = (
        w_in[:, edges[i]:edges[i + 1]] for i in range(7))
    w_r = jnp.concatenate([fq, dq, fk, dk, fv, dv], axis=1)
    w_gate = jnp.pad(fgate, ((0, 0), (0, LANES - N_FOX_HEADS)))
    return w_r.astype(jnp.bfloat16), w_gate.astype(jnp.bfloat16)


def kernel(x, ffn1_pre_g, ffn1_post_g, ffn1_w_gate, ffn1_w_up, ffn1_w_down, mix_pre_g, mix_post_g, w_in, fox_forget_b, diff_lambda_q1, diff_lambda_k1, diff_lambda_q2, diff_lambda_k2, diff_subln_g, w_out, ffn2_pre_g, ffn2_post_g, ffn2_w_gate, ffn2_w_up, ffn2_w_down):
    batch, seq, d = x.shape
    bf = lambda w: w.astype(jnp.bfloat16)
    x2d = x.reshape(batch * seq, d)
    cos, sup, sdn = _rope_tables(seq)
    for l in range(ffn1_pre_g.shape[0]):
        x2d = _ffn(x2d, ffn1_pre_g[l][None], ffn1_post_g[l][None],
                   bf(ffn1_w_gate[l]), bf(ffn1_w_up[l]), bf(ffn1_w_down[l]))

        w_r, w_gate = _relayout_w_in(w_in[l])
        proj, gate, nsq = _in_proj(x2d, mix_pre_g[l][None], w_r, w_gate,
                                   cos, sup, sdn, seq)
        bias_row = jnp.pad(fox_forget_b[l], (0, LANES - N_FOX_HEADS))[None]
        ccol, crow = _decay(gate, bias_row, batch, seq)

        fox_o = _attention(
            proj, batch, seq, 0, _fox_kernel,
            [pl.BlockSpec((1, seq, LANES), lambda b, i: (b, 0, 0)),
             pl.BlockSpec((1, N_FOX_HEADS, seq), lambda b, i: (b, 0, 0)),
             pl.BlockSpec((seq // PROJ_SUB * NSQ_ROWS, LANES),
                          lambda b, i: (b, 0))],
            [ccol, crow, nsq],
            [pltpu.VMEM((N_FOX_HEADS, seq, LANES), jnp.float32),
             pltpu.VMEM((1, LANES), jnp.float32)], "fox_attn")
        lam_rows = jnp.pad(
            jnp.stack([diff_lambda_q1[l], diff_lambda_k1[l],
                       diff_lambda_q2[l], diff_lambda_k2[l]]),
            ((0, 4), (0, LANES - HEAD_DIM)))
        diff_o = _attention(
            proj, batch, seq, 1, _diff_kernel,
            [_const_spec((8, LANES)), _const_spec((1, PAIR))],
            [lam_rows, diff_subln_g[l][None]], [], "diff_attn")

        x2d = _mix_ffn(x2d, fox_o, diff_o, bf(w_out[l]), mix_post_g[l][None],
                       ffn2_pre_g[l][None], ffn2_post_g[l][None],
                       bf(ffn2_w_gate[l]), bf(ffn2_w_up[l]), bf(ffn2_w_down[l]))
    return x2d.reshape(batch, seq, d)
```

```python
import math

import jax
import jax.numpy as jnp
import numpy as np
from jax import lax
from jax.experimental import pallas as pl
from jax.experimental.pallas import tpu as pltpu

D_MODEL = 1024
N_FOX_HEADS = 8
N_DIFF_HEADS = 4
HEAD_DIM = 64
PAIR = 2 * HEAD_DIM
FOX_WIDTH = N_FOX_HEADS * HEAD_DIM
DIFF_WIDTH = N_DIFF_HEADS * PAIR
ROPE_THETA = 500000.0
ROPE_DIM = HEAD_DIM // 4
ROPE_HALF = ROPE_DIM // 2
D_FF = 2816
FFN_RESIDUAL_WEIGHT = 0.5
RMS_EPS = 1e-6
LAMBDA_INIT = 0.8 - 0.6 * math.exp(-0.3 * 0)
QK_SCALE = HEAD_DIM ** -0.5
LOG2E = math.log2(math.e)

LANES = 128
MASK_VALUE = -1e30
VMEM_LIMIT_BYTES = 56 * 1024 * 1024

GROUP = FOX_WIDTH
PROJ_WIDTH = 6 * GROUP
UNITS = GROUP // PAIR


def _rms(x, g):
    return x * lax.rsqrt(jnp.mean(x * x, axis=-1, keepdims=True) + RMS_EPS) * g


def _const_spec(shape):
    return pl.BlockSpec(shape, lambda *_: (0,) * len(shape),
                        pipeline_mode=pl.Buffered(1))


FFN_TM = 1024
FFN_SUB = 256
FFN_CHUNKS = ((0, 1536), (1536, 1280))


def _swiglu_half_step(x, pre_g, post_g, wg_ref, wu_ref, wd_ref):
    xn = _rms(x, pre_g).astype(jnp.bfloat16)
    acc = None
    for start, width in FFN_CHUNKS:
        g = jnp.dot(xn, wg_ref[:, start:start + width],
                    preferred_element_type=jnp.float32)
        u = jnp.dot(xn, wu_ref[:, start:start + width],
                    preferred_element_type=jnp.float32)
        h = (g * jax.nn.sigmoid(g) * u).astype(jnp.bfloat16)
        part = jnp.dot(h, wd_ref[start:start + width, :],
                       preferred_element_type=jnp.float32)
        acc = part if acc is None else acc + part
    return x + FFN_RESIDUAL_WEIGHT * _rms(acc, post_g)


def _ffn_kernel(x_ref, pre_g_ref, post_g_ref, wg_ref, wu_ref, wd_ref, o_ref):
    for r in range(0, FFN_TM, FFN_SUB):
        o_ref[r:r + FFN_SUB, :] = _swiglu_half_step(
            x_ref[r:r + FFN_SUB, :], pre_g_ref[...], post_g_ref[...],
            wg_ref, wu_ref, wd_ref)


def _mix_ffn_kernel(x_ref, fox_ref, diff_ref, wout_ref, mix_g_ref,
                    pre_g_ref, post_g_ref, wg_ref, wu_ref, wd_ref, o_ref):
    def mixed(r):
        rows = slice(r, r + FFN_SUB)
        m = jnp.dot(fox_ref[rows, :], wout_ref[:FOX_WIDTH, :],
                    preferred_element_type=jnp.float32)
        m = m + jnp.dot(diff_ref[rows, :], wout_ref[FOX_WIDTH:, :],
                        preferred_element_type=jnp.float32)
        return x_ref[rows, :] + _rms(m, mix_g_ref[...])

    x1_next = mixed(0)
    for r in range(0, FFN_TM, FFN_SUB):
        x1 = x1_next
        if r + FFN_SUB < FFN_TM:
            x1_next = mixed(r + FFN_SUB)
        o_ref[r:r + FFN_SUB, :] = _swiglu_half_step(
            x1, pre_g_ref[...], post_g_ref[...], wg_ref, wu_ref, wd_ref)


def _ffn_weight_specs():
    return [_const_spec((1, D_MODEL)), _const_spec((1, D_MODEL)),
            _const_spec((D_MODEL, D_FF)), _const_spec((D_MODEL, D_FF)),
            _const_spec((D_FF, D_MODEL))]


def _ffn(x2d, pre_g, post_g, wg, wu, wd):
    t = x2d.shape[0]
    row = pl.BlockSpec((FFN_TM, D_MODEL), lambda i: (i, 0))
    return pl.pallas_call(
        _ffn_kernel,
        grid=(t // FFN_TM,),
        in_specs=[row] + _ffn_weight_specs(),
        out_specs=row,
        out_shape=jax.ShapeDtypeStruct(x2d.shape, x2d.dtype),
        compiler_params=pltpu.CompilerParams(
            dimension_semantics=("arbitrary",),
            vmem_limit_bytes=VMEM_LIMIT_BYTES),
        name="ffn",
    )(x2d, pre_g, post_g, wg, wu, wd)


def _mix_ffn(x2d, fox_o, diff_o, w_out, mix_g, pre_g, post_g, wg, wu, wd):
    t = x2d.shape[0]
    row = lambda w: pl.BlockSpec((FFN_TM, w), lambda i: (i, 0))
    return pl.pallas_call(
        _mix_ffn_kernel,
        grid=(t // FFN_TM,),
        in_specs=[row(D_MODEL), row(FOX_WIDTH), row(DIFF_WIDTH),
                  _const_spec((FOX_WIDTH + DIFF_WIDTH, D_MODEL)),
                  _const_spec((1, D_MODEL))] + _ffn_weight_specs(),
        out_specs=row(D_MODEL),
        out_shape=jax.ShapeDtypeStruct(x2d.shape, x2d.dtype),
        compiler_params=pltpu.CompilerParams(
            dimension_semantics=("arbitrary",),
            vmem_limit_bytes=VMEM_LIMIT_BYTES),
        name="mix_ffn",
    )(x2d, fox_o, diff_o, w_out, mix_g, pre_g, post_g, wg, wu, wd)


PROJ_TM = 1024
PROJ_SUB = 256
ROPE_GROUPS = (1, 3)
Q_GROUPS = (0, 1)
FOX_QK_GROUPS = (0, 2)
NSQ_ROWS = 2 * UNITS


def _rope_slab(y, cos, sin_up, sin_dn):
    up = pltpu.roll(y, ROPE_HALF, 1)
    dn = pltpu.roll(y, LANES - ROPE_HALF, 1)
    return y * cos + up * sin_up + dn * sin_dn


def _max_head_sqnorms(slab):
    lane = lax.broadcasted_iota(jnp.int32, slab.shape, 1)
    sq = slab * slab
    head = lambda keep: jnp.max(
        jnp.sum(jnp.where(keep, sq, 0.0), axis=1, keepdims=True),
        axis=0, keepdims=True)
    lane1 = lax.broadcasted_iota(jnp.int32, (1, LANES), 1)
    return jnp.where(lane1 == 0, head(lane < HEAD_DIM),
                     jnp.where(lane1 == 1, head(lane >= HEAD_DIM), 0.0))


def _in_proj_kernel(x_ref, g_ref, w_ref, wgate_ref, cos_ref, sup_ref, sdn_ref,
                    proj_ref, gate_ref, nsq_ref):
    for r in range(0, PROJ_TM, PROJ_SUB):
        rows = slice(r, r + PROJ_SUB)
        xn = _rms(x_ref[rows, :], g_ref[...]).astype(jnp.bfloat16)
        cos, sup, sdn = cos_ref[rows, :], sup_ref[rows, :], sdn_ref[rows, :]
        sqnorms = []
        for c in range(PROJ_WIDTH // GROUP):
            lo = c * GROUP
            y = jnp.dot(xn, w_ref[:, lo:lo + GROUP],
                        preferred_element_type=jnp.float32)
            for s in range(UNITS):
                slab = y[:, s * LANES:(s + 1) * LANES]
                if c in ROPE_GROUPS:
                    slab = _rope_slab(slab, cos, sup, sdn)
                if c in Q_GROUPS:
                    slab = slab * (QK_SCALE * LOG2E)
                slab = slab.astype(proj_ref.dtype)
                proj_ref[rows, lo + s * LANES:lo + (s + 1) * LANES] = slab
                if c in FOX_QK_GROUPS:
                    sqnorms.append(_max_head_sqnorms(slab.astype(jnp.float32)))
        gate_ref[rows, :] = jnp.dot(xn, wgate_ref[...],
                                    preferred_element_type=jnp.float32)
        n0 = (r // PROJ_SUB) * NSQ_ROWS
        nsq_ref[n0:n0 + NSQ_ROWS, :] = jnp.concatenate(sqnorms, axis=0)


def _in_proj(x2d, pre_g, w_r, w_gate, cos, sup, sdn, seq):
    t = x2d.shape[0]
    tiles_per_seq = seq // PROJ_TM
    row = lambda w: pl.BlockSpec((PROJ_TM, w), lambda i: (i, 0))
    table = pl.BlockSpec((PROJ_TM, LANES), lambda i: (i % tiles_per_seq, 0))
    nsq_rows = PROJ_TM // PROJ_SUB * NSQ_ROWS
    return pl.pallas_call(
        _in_proj_kernel,
        grid=(t // PROJ_TM,),
        in_specs=[row(D_MODEL), _const_spec((1, D_MODEL)),
                  _const_spec((D_MODEL, PROJ_WIDTH)),
                  _const_spec((D_MODEL, LANES)), table, table, table],
        out_specs=[row(PROJ_WIDTH), row(LANES),
                   pl.BlockSpec((nsq_rows, LANES), lambda i: (i, 0))],
        out_shape=[jax.ShapeDtypeStruct((t, PROJ_WIDTH), jnp.bfloat16),
                   jax.ShapeDtypeStruct((t, LANES), jnp.float32),
                   jax.ShapeDtypeStruct((t // PROJ_TM * nsq_rows, LANES),
                                        jnp.float32)],
        compiler_params=pltpu.CompilerParams(
            dimension_semantics=("arbitrary",),
            vmem_limit_bytes=VMEM_LIMIT_BYTES),
        name="in_proj",
    )(x2d, pre_g, w_r, w_gate, cos, sup, sdn)


C_PIECES = 3
C_LANES = N_FOX_HEADS * C_PIECES


def _decay_kernel(gate_ref, b_ref, piece_ref, ccol_ref, crow_ref, cpiece_ref):
    z = gate_ref[0] + b_ref[...]
    c = jnp.minimum(z, 0.0) - jnp.log1p(jnp.exp(-jnp.abs(z)))
    seq = c.shape[0]
    rows = lax.broadcasted_iota(jnp.int32, c.shape, 0)
    shift = 1
    while shift < seq:
        c = c + jnp.where(rows >= shift, pltpu.roll(c, shift, 0), 0.0)
        shift *= 2
    c = c * LOG2E
    ccol_ref[0] = c
    crow_ref[0] = c.T[:C_LANES, :]
    hi = c.astype(jnp.bfloat16)
    rest = c - hi.astype(jnp.float32)
    mid = rest.astype(jnp.bfloat16)
    lo = (rest - mid.astype(jnp.float32)).astype(jnp.bfloat16)
    piece = piece_ref[...]
    cpiece_ref[0] = jnp.where(piece == 0, hi, jnp.where(piece == 1, mid, lo))


def _decay(gate, bias_row, piece_row, batch, seq):
    gate3 = gate.reshape(batch, seq, LANES)
    col = pl.BlockSpec((1, seq, LANES), lambda b: (b, 0, 0))
    row = pl.BlockSpec((1, C_LANES, seq), lambda b: (b, 0, 0))
    return pl.pallas_call(
        _decay_kernel,
        grid=(batch,),
        in_specs=[col, _const_spec((1, LANES)), _const_spec((1, LANES))],
        out_specs=[col, row, col],
        out_shape=[jax.ShapeDtypeStruct((batch, seq, LANES), jnp.float32),
                   jax.ShapeDtypeStruct((batch, C_LANES, seq), jnp.float32),
                   jax.ShapeDtypeStruct((batch, seq, LANES), jnp.bfloat16)],
        compiler_params=pltpu.CompilerParams(
            dimension_semantics=("arbitrary",),
            vmem_limit_bytes=VMEM_LIMIT_BYTES),
        name="decay",
    )(gate3, bias_row, piece_row)


ATT_TQ = 512
ATT_HALF = ATT_TQ // 2
ATT_TK = ATT_HALF
ONES_ROWS = 16
VT_ROWS = PAIR + ONES_ROWS
SKIP_LOG2 = 160.0
BOUND_SLACK = 1.02


def _logit_bound(nsq_ref):
    n = nsq_ref[0:NSQ_ROWS, :]
    for r in range(NSQ_ROWS, nsq_ref.shape[0], NSQ_ROWS):
        n = jnp.maximum(n, nsq_ref[r:r + NSQ_ROWS, :])
    bound = jnp.sqrt(n[:UNITS] * n[UNITS:]) * BOUND_SLACK
    lane = lax.broadcasted_iota(jnp.int32, (1, LANES), 1)
    out = jnp.zeros((1, LANES), jnp.float32)
    for h in range(N_FOX_HEADS):
        value = jnp.sum(jnp.where(lane == h % 2, bound[h // 2:h // 2 + 1], 0.0),
                        axis=1, keepdims=True)
        mine = (lane >= C_PIECES * h) & (lane < C_PIECES * (h + 1))
        out = jnp.where(mine, value, out)
    return out


def _first_needed_block(ccol_ref, bh_scr, q0, tk, limit):
    n_blocks = ccol_ref.shape[1] // tk
    c_end = ccol_ref[0, pl.ds(tk - 1, n_blocks, stride=tk), :]
    c_q = ccol_ref[0, pl.ds(q0, 1), :]
    bound = 2.0 * bh_scr[...] + (c_q - c_end)
    block = lax.broadcasted_iota(jnp.int32, bound.shape, 0)
    lane = lax.broadcasted_iota(jnp.int32, bound.shape, 1)
    needed = (lane < C_LANES) & jnp.logical_not(bound < -SKIP_LOG2)
    first = jnp.min(jnp.where(needed, block, n_blocks).astype(jnp.float32))
    return jnp.minimum(first.astype(jnp.int32), limit)


def _decay_selector(u, width, th):
    row = lax.broadcasted_iota(jnp.int32, (LANES, width), 0)
    col = lax.broadcasted_iota(jnp.int32, (LANES, width), 1)
    second_map = (col & th) != 0
    first = C_PIECES * 2 * u
    owner_a = (row >= first) & (row < first + C_PIECES)
    owner_b = (row >= first + C_PIECES) & (row < first + 2 * C_PIECES)
    hit = (second_map & owner_b) | (jnp.logical_not(second_map) & owner_a)
    return jnp.where(hit, -1.0, 0.0).astype(jnp.bfloat16)


def _attn_body(is_fox, q_ref, k_ref, v_ref, ccol_ref, crow_ref, cpiece_ref,
               nsq_ref, lam_ref, g_ref, o_ref, m_scr, l_scr, acc_scr, qt_scr,
               s0_scr, s1_scr, vt_scr, bh_scr):
    tq, th, tk = ATT_TQ, ATT_HALF, ATT_TK
    qi = pl.program_id(1)
    q0 = pl.multiple_of(qi * tq, tq)

    @pl.when(qi == 0)
    def _():
        seq = v_ref.shape[0]
        for u in range(UNITS):
            vt_scr[u, :PAIR, :] = v_ref[:, u * PAIR:(u + 1) * PAIR].T
            vt_scr[u, PAIR:, :] = jnp.ones((ONES_ROWS, seq), vt_scr.dtype)
            if is_fox:
                qt_scr[u, PAIR:, :] = _decay_selector(u, 2 * tq, th)
        if is_fox:
            bh_scr[...] = _logit_bound(nsq_ref)

    first_diag = 2 * qi
    j0 = (_first_needed_block(ccol_ref, bh_scr, q0, tk, first_diag)
          if is_fox else 0)
    n_full = first_diag - j0

    lane = lax.broadcasted_iota(jnp.int32, (th, LANES), 1)
    ct = []
    for u in range(UNITS):
        pieces, ct_u = [], []
        for h in range(2):
            q = q_ref[h * th:(h + 1) * th, u * PAIR:(u + 1) * PAIR]
            zero = jnp.zeros_like(q)
            pieces += [jnp.where(lane < HEAD_DIM, q, zero),
                       jnp.where(lane >= HEAD_DIM, q, zero)]
            if is_fox:
                t0 = pl.multiple_of(q0 + h * th, th)
                ct_u += [crow_ref[0, pl.ds(C_PIECES * (2 * u + m), 1),
                                  pl.ds(t0, th)] for m in range(2)]
        qt_scr[u, :PAIR, :] = jnp.concatenate(pieces, axis=0).T
        if is_fox:
            ct.append(jnp.concatenate(ct_u, axis=1))

    m_scr[...] = jnp.full_like(m_scr, MASK_VALUE)
    l_scr[...] = jnp.zeros_like(l_scr)
    acc_scr[...] = jnp.zeros_like(acc_scr)

    def logits_to(slot, j, u, lo=0):
        k0 = pl.multiple_of(j * tk, tk)
        kb = k_ref[pl.ds(k0, tk), u * PAIR:(u + 1) * PAIR]
        if is_fox:
            kb = jnp.concatenate([kb, cpiece_ref[0, pl.ds(k0, tk), :]], axis=1)
        slot[u, :, lo:] = jnp.dot(kb, qt_scr[u, :, lo:],
                                  preferred_element_type=jnp.float32)

    def softmax_pv(slot, j, u, lo, masked):
        k0 = pl.multiple_of(j * tk, tk)
        s = slot[u, :, lo:]
        width = s.shape[1]
        if masked:
            r = lax.broadcasted_iota(jnp.int32, (tk, width), 0)
            c = lax.broadcasted_iota(jnp.int32, (tk, width), 1)
            keep = r <= (c & (th - 1))
            if lo == 0:
                keep = keep | (c >= tq)
            s = jnp.where(keep, s, MASK_VALUE)
        m_prev = m_scr[u, :, lo:]
        col_max = jnp.max(s, axis=0, keepdims=True)
        if is_fox:
            m_next = jnp.maximum(m_prev, col_max + ct[u][:, lo:])
            offset = m_next - ct[u][:, lo:]
        else:
            m_next = jnp.maximum(m_prev, col_max)
            offset = m_next
        p = jnp.exp2(s - offset).astype(vt_scr.dtype)
        alpha = jnp.exp2(m_prev - m_next)
        vt = vt_scr[u, :, pl.ds(k0, tk)]
        pv = jnp.dot(vt, p, preferred_element_type=jnp.float32)
        l_scr[u, :, lo:] = alpha * l_scr[u, :, lo:] + pv[PAIR:PAIR + 1, :]
        acc_scr[u, :, lo:] = alpha * acc_scr[u, :, lo:] + pv[:PAIR, :]
        m_scr[u, :, lo:] = m_next

    def step(j, cur, nxt, lo=0, masked=False, nxt_lo=0):
        ahead = 1
        if nxt is not None:
            for u in range(ahead):
                logits_to(nxt, j + 1, u, nxt_lo)
        for u in range(UNITS):
            softmax_pv(cur, j, u, lo, masked)
            if nxt is not None and u + ahead < UNITS:
                logits_to(nxt, j + 1, u + ahead, nxt_lo)

    for u in range(UNITS):
        logits_to(s0_scr, j0, u)

    def pair(i, carry):
        step(j0 + 2 * i, s0_scr, s1_scr)
        step(j0 + 2 * i + 1, s1_scr, s0_scr)
        return carry

    lax.fori_loop(0, lax.shift_right_logical(n_full, 1), pair, 0)

    def diagonal(cur, other):
        step(first_diag, cur, other, masked=True, nxt_lo=tq)
        step(first_diag + 1, other, None, lo=tq, masked=True)

    if is_fox:
        @pl.when((n_full & 1) == 0)
        def _():
            diagonal(s0_scr, s1_scr)

        @pl.when((n_full & 1) == 1)
        def _():
            step(first_diag - 1, s0_scr, s1_scr)
            diagonal(s1_scr, s0_scr)
    else:
        diagonal(s0_scr, s1_scr)

    if not is_fox:
        lam_rows = lam_ref[...]
        dot = lambda i: jnp.sum(lam_rows[i:i + 1] * lam_rows[i + 1:i + 2],
                                axis=1, keepdims=True)
        lam = jnp.exp(dot(0)) - jnp.exp(dot(2)) + LAMBDA_INIT
    for u in range(UNITS):
        o_t = acc_scr[u] / l_scr[u]
        for h in range(2):
            a = o_t[:, 2 * h * th:(2 * h + 1) * th]
            b = o_t[:, (2 * h + 1) * th:(2 * h + 2) * th]
            if is_fox:
                out = jnp.concatenate([a[:HEAD_DIM], b[HEAD_DIM:]], axis=0).T
            else:
                d_t = a - lam * b
                inv = lax.rsqrt(jnp.mean(d_t * d_t, axis=0, keepdims=True)
                                + RMS_EPS)
                out = (d_t * inv).T * g_ref[...] * (1.0 - LAMBDA_INIT)
            o_ref[h * th:(h + 1) * th, u * PAIR:(u + 1) * PAIR] = out.astype(
                o_ref.dtype)


def _fox_kernel(q_ref, k_ref, v_ref, ccol_ref, crow_ref, cpiece_ref, nsq_ref,
                o_ref, *scratch):
    _attn_body(True, q_ref, k_ref, v_ref, ccol_ref, crow_ref, cpiece_ref,
               nsq_ref, None, None, o_ref, *scratch)


def _diff_kernel(q_ref, k_ref, v_ref, lam_ref, g_ref, o_ref, *scratch):
    _attn_body(False, q_ref, k_ref, v_ref, None, None, None, None, lam_ref,
               g_ref, o_ref, *scratch, None)


def _attention(proj, batch, seq, group, kernel, extra_specs, extra_args,
               extra_scratch, name):
    nq = seq // ATT_TQ
    q_spec = pl.BlockSpec((ATT_TQ, GROUP), lambda b, i: (b * nq + i, group))
    kv_spec = lambda g: pl.BlockSpec((seq, GROUP), lambda b, i: (b, g))
    out_spec = pl.BlockSpec((ATT_TQ, GROUP), lambda b, i: (b * nq + i, 0))
    stat = pltpu.VMEM((UNITS, 1, 2 * ATT_TQ), jnp.float32)
    acc = pltpu.VMEM((UNITS, PAIR, 2 * ATT_TQ), jnp.float32)
    depth = PAIR + (LANES if group == 0 else 0)
    q_t = pltpu.VMEM((UNITS, depth, 2 * ATT_TQ), jnp.bfloat16)
    logit_slot = pltpu.VMEM((UNITS, ATT_TK, 2 * ATT_TQ), jnp.float32)
    v_t = pltpu.VMEM((UNITS, VT_ROWS, seq), jnp.bfloat16)
    return pl.pallas_call(
        kernel,
        grid=(batch, nq),
        in_specs=[q_spec, kv_spec(2 + group), kv_spec(4 + group)] + extra_specs,
        out_specs=out_spec,
        out_shape=jax.ShapeDtypeStruct((batch * seq, GROUP), jnp.bfloat16),
        scratch_shapes=[stat, stat, acc, q_t, logit_slot, logit_slot, v_t]
        + extra_scratch,
        compiler_params=pltpu.CompilerParams(
            dimension_semantics=("arbitrary", "arbitrary"),
            vmem_limit_bytes=VMEM_LIMIT_BYTES),
        name=name,
    )(proj, proj, proj, *extra_args)


def _rope_tables(seq):
    inv_freq = ROPE_THETA ** (-jnp.arange(0, ROPE_DIM, 2, dtype=jnp.float32)
                              / ROPE_DIM)
    ang = jnp.arange(seq).astype(jnp.float32)[:, None] * inv_freq[None, :]
    cos, sin = jnp.cos(ang), jnp.sin(ang)
    ones = jnp.ones((seq, HEAD_DIM - ROPE_DIM), jnp.float32)
    zeros = jnp.zeros((seq, HEAD_DIM - ROPE_DIM), jnp.float32)
    z8 = jnp.zeros((seq, ROPE_HALF), jnp.float32)
    cos64 = jnp.concatenate([cos, cos, ones], axis=1)
    up64 = jnp.concatenate([z8, sin, zeros], axis=1)
    dn64 = jnp.concatenate([-sin, z8, zeros], axis=1)
    two = lambda a: jnp.concatenate([a, a], axis=1)
    return two(cos64), two(up64), two(dn64)


def _decay_lanes(per_head):
    rep = jnp.repeat(per_head, C_PIECES, axis=-1)
    pad = [(0, 0)] * (rep.ndim - 1) + [(0, LANES - C_LANES)]
    return jnp.pad(rep, pad)


def _relayout_w_in(w_in):
    fw, dw = FOX_WIDTH, DIFF_WIDTH
    edges = np.cumsum([0, fw, fw, fw, N_FOX_HEADS, dw, dw, dw])
    w16 = w_in.astype(jnp.bfloat16)
    fq, fk, fv, fgate, dq, dk, dv = (
        w16[:, edges[i]:edges[i + 1]] for i in range(7))
    w_r = jnp.concatenate([fq, dq, fk, dk, fv, dv], axis=1)
    return w_r, _decay_lanes(fgate)


def kernel(x, ffn1_pre_g, ffn1_post_g, ffn1_w_gate, ffn1_w_up, ffn1_w_down, mix_pre_g, mix_post_g, w_in, fox_forget_b, diff_lambda_q1, diff_lambda_k1, diff_lambda_q2, diff_lambda_k2, diff_subln_g, w_out, ffn2_pre_g, ffn2_post_g, ffn2_w_gate, ffn2_w_up, ffn2_w_down):
    batch, seq, d = x.shape
    bf = lambda w: w.astype(jnp.bfloat16)
    x2d = x.reshape(batch * seq, d)
    cos, sup, sdn = _rope_tables(seq)
    for l in range(ffn1_pre_g.shape[0]):
        x2d = _ffn(x2d, ffn1_pre_g[l][None], ffn1_post_g[l][None],
                   bf(ffn1_w_gate[l]), bf(ffn1_w_up[l]), bf(ffn1_w_down[l]))

        w_r, w_gate = _relayout_w_in(w_in[l])
        proj, gate, nsq = _in_proj(x2d, mix_pre_g[l][None], w_r, w_gate,
                                   cos, sup, sdn, seq)
        piece_row = jnp.asarray(np.arange(LANES) % C_PIECES, jnp.int32)[None]
        ccol, crow, cpiece = _decay(gate, _decay_lanes(fox_forget_b[l])[None],
                                    piece_row, batch, seq)

        per_batch_col = pl.BlockSpec((1, seq, LANES), lambda b, i: (b, 0, 0))
        fox_o = _attention(
            proj, batch, seq, 0, _fox_kernel,
            [per_batch_col,
             pl.BlockSpec((1, C_LANES, seq), lambda b, i: (b, 0, 0)),
             per_batch_col,
             pl.BlockSpec((seq // PROJ_SUB * NSQ_ROWS, LANES),
                          lambda b, i: (b, 0))],
            [ccol, crow, cpiece, nsq],
            [pltpu.VMEM((1, LANES), jnp.float32)], "fox_attn")
        lam_rows = jnp.pad(
            jnp.stack([diff_lambda_q1[l], diff_lambda_k1[l],
                       diff_lambda_q2[l], diff_lambda_k2[l]]),
            ((0, 4), (0, LANES - HEAD_DIM)))
        diff_o = _attention(
            proj, batch, seq, 1, _diff_kernel,
            [_const_spec((8, LANES)), _const_spec((1, PAIR))],
            [lam_rows, diff_subln_g[l][None]], [], "diff_attn")

        x2d = _mix_ffn(x2d, fox_o, diff_o, bf(w_out[l]), mix_post_g[l][None],
                       ffn2_pre_g[l][None], ffn2_post_g[l][None],
                       bf(ffn2_w_gate[l]), bf(ffn2_w_up[l]), bf(ffn2_w_down[l]))
    return x2d.reshape(batch, seq, d)
```

```python
import math

import jax
import jax.numpy as jnp
import numpy as np
from jax import lax
from jax.experimental import pallas as pl
from jax.experimental.pallas import tpu as pltpu

D_MODEL = 1024
N_FOX_HEADS = 8
N_DIFF_HEADS = 4
HEAD_DIM = 64
PAIR = 2 * HEAD_DIM
FOX_WIDTH = N_FOX_HEADS * HEAD_DIM
DIFF_WIDTH = N_DIFF_HEADS * PAIR
ROPE_THETA = 500000.0
ROPE_DIM = HEAD_DIM // 4
ROPE_HALF = ROPE_DIM // 2
D_FF = 2816
FFN_RESIDUAL_WEIGHT = 0.5
RMS_EPS = 1e-6
LAMBDA_INIT = 0.8 - 0.6 * math.exp(-0.3 * 0)
QK_SCALE = HEAD_DIM ** -0.5
LOG2E = math.log2(math.e)

LANES = 128
MASK_VALUE = -1e30
VMEM_LIMIT_BYTES = 56 * 1024 * 1024

GROUP = FOX_WIDTH
PROJ_WIDTH = 6 * GROUP
UNITS = GROUP // PAIR


def _rms(x, g):
    return x * lax.rsqrt(jnp.mean(x * x, axis=-1, keepdims=True) + RMS_EPS) * g


def _const_spec(shape):
    return pl.BlockSpec(shape, lambda *_: (0,) * len(shape),
                        pipeline_mode=pl.Buffered(1))


FFN1_TM = 1024
FFN_TM = 1024
FFN_SUB = 256
FFN_CHUNKS = ((0, 1536), (1536, 1280))


def _swiglu_half_step(x, pre_g, post_g, wg_ref, wu_ref, wd_ref):
    xn = _rms(x, pre_g).astype(jnp.bfloat16)
    acc = None
    for start, width in FFN_CHUNKS:
        g = jnp.dot(xn, wg_ref[:, start:start + width],
                    preferred_element_type=jnp.float32)
        u = jnp.dot(xn, wu_ref[:, start:start + width],
                    preferred_element_type=jnp.float32)
        h = (g * jax.nn.sigmoid(g) * u).astype(jnp.bfloat16)
        part = jnp.dot(h, wd_ref[start:start + width, :],
                       preferred_element_type=jnp.float32)
        acc = part if acc is None else acc + part
    return x + FFN_RESIDUAL_WEIGHT * _rms(acc, post_g)


def _ffn_kernel(x_ref, pre_g_ref, post_g_ref, wg_ref, wu_ref, wd_ref, o_ref):
    for r in range(0, FFN1_TM, FFN_SUB):
        o_ref[r:r + FFN_SUB, :] = _swiglu_half_step(
            x_ref[r:r + FFN_SUB, :], pre_g_ref[...], post_g_ref[...],
            wg_ref, wu_ref, wd_ref)


def _mix_ffn_kernel(x_ref, fox_ref, diff_ref, wout_ref, mix_g_ref,
                    pre_g_ref, post_g_ref, wg_ref, wu_ref, wd_ref, o_ref):
    def mixed(r):
        rows = slice(r, r + FFN_SUB)
        m = jnp.dot(fox_ref[rows, :], wout_ref[:FOX_WIDTH, :],
                    preferred_element_type=jnp.float32)
        m = m + jnp.dot(diff_ref[rows, :], wout_ref[FOX_WIDTH:, :],
                        preferred_element_type=jnp.float32)
        return x_ref[rows, :] + _rms(m, mix_g_ref[...])

    x1_next = mixed(0)
    for r in range(0, FFN_TM, FFN_SUB):
        x1 = x1_next
        if r + FFN_SUB < FFN_TM:
            x1_next = mixed(r + FFN_SUB)
        o_ref[r:r + FFN_SUB, :] = _swiglu_half_step(
            x1, pre_g_ref[...], post_g_ref[...], wg_ref, wu_ref, wd_ref)


def _ffn_weight_specs():
    return [_const_spec((1, D_MODEL)), _const_spec((1, D_MODEL)),
            _const_spec((D_MODEL, D_FF)), _const_spec((D_MODEL, D_FF)),
            _const_spec((D_FF, D_MODEL))]


def _ffn(x2d, pre_g, post_g, wg, wu, wd):
    t = x2d.shape[0]
    row = pl.BlockSpec((FFN1_TM, D_MODEL), lambda i: (i, 0))
    return pl.pallas_call(
        _ffn_kernel,
        grid=(t // FFN1_TM,),
        in_specs=[row] + _ffn_weight_specs(),
        out_specs=row,
        out_shape=jax.ShapeDtypeStruct(x2d.shape, x2d.dtype),
        compiler_params=pltpu.CompilerParams(
            dimension_semantics=("arbitrary",),
            vmem_limit_bytes=VMEM_LIMIT_BYTES),
        name="ffn",
    )(x2d, pre_g, post_g, wg, wu, wd)


def _mix_ffn(x2d, fox_o, diff_o, w_out, mix_g, pre_g, post_g, wg, wu, wd):
    t = x2d.shape[0]
    row = lambda w: pl.BlockSpec((FFN_TM, w), lambda i: (i, 0))
    return pl.pallas_call(
        _mix_ffn_kernel,
        grid=(t // FFN_TM,),
        in_specs=[row(D_MODEL), row(FOX_WIDTH), row(DIFF_WIDTH),
                  _const_spec((FOX_WIDTH + DIFF_WIDTH, D_MODEL)),
                  _const_spec((1, D_MODEL))] + _ffn_weight_specs(),
        out_specs=row(D_MODEL),
        out_shape=jax.ShapeDtypeStruct(x2d.shape, x2d.dtype),
        compiler_params=pltpu.CompilerParams(
            dimension_semantics=("arbitrary",),
            vmem_limit_bytes=VMEM_LIMIT_BYTES),
        name="mix_ffn",
    )(x2d, fox_o, diff_o, w_out, mix_g, pre_g, post_g, wg, wu, wd)


PROJ_TM = 1024
PROJ_SUB = 256
ROPE_GROUPS = (1, 3)
Q_GROUPS = (0, 1)
FOX_QK_GROUPS = (0, 2)
_DIFF_START = 3 * FOX_WIDTH + N_FOX_HEADS
W_IN_GROUP_STARTS = (0, _DIFF_START, FOX_WIDTH, _DIFF_START + DIFF_WIDTH,
                     2 * FOX_WIDTH, _DIFF_START + 2 * DIFF_WIDTH)
W_CAST_ROWS = 128
NSQ_ROWS = 2 * UNITS


def _rope_slab(y, cos, sin_up, sin_dn):
    up = pltpu.roll(y, ROPE_HALF, 1)
    dn = pltpu.roll(y, LANES - ROPE_HALF, 1)
    return y * cos + up * sin_up + dn * sin_dn


def _max_head_sqnorms(slab):
    lane = lax.broadcasted_iota(jnp.int32, slab.shape, 1)
    sq = slab * slab
    head = lambda keep: jnp.max(
        jnp.sum(jnp.where(keep, sq, 0.0), axis=1, keepdims=True),
        axis=0, keepdims=True)
    lane1 = lax.broadcasted_iota(jnp.int32, (1, LANES), 1)
    return jnp.where(lane1 == 0, head(lane < HEAD_DIM),
                     jnp.where(lane1 == 1, head(lane >= HEAD_DIM), 0.0))


def _in_proj_kernel(x_ref, g_ref, win_ref, wgate_ref, rope_ref,
                    proj_ref, gate_ref, nsq_ref, w_scr):
    @pl.when(pl.program_id(0) == 0)
    def _():
        for dst, src in enumerate(W_IN_GROUP_STARTS):
            for r in range(0, D_MODEL, W_CAST_ROWS):
                w_scr[r:r + W_CAST_ROWS, dst * GROUP:(dst + 1) * GROUP] = (
                    win_ref[r:r + W_CAST_ROWS, src:src + GROUP].astype(
                        w_scr.dtype))

    for r in range(0, PROJ_TM, PROJ_SUB):
        rows = slice(r, r + PROJ_SUB)
        xn = _rms(x_ref[rows, :], g_ref[...]).astype(jnp.bfloat16)
        cos, sup, sdn = (rope_ref[rows, t * LANES:(t + 1) * LANES]
                         for t in range(3))
        sqnorms = []
        for c in range(PROJ_WIDTH // GROUP):
            lo = c * GROUP
            y = jnp.dot(xn, w_scr[:, lo:lo + GROUP],
                        preferred_element_type=jnp.float32)
            for s in range(UNITS):
                slab = y[:, s * LANES:(s + 1) * LANES]
                if c in ROPE_GROUPS:
                    slab = _rope_slab(slab, cos, sup, sdn)
                if c in Q_GROUPS:
                    slab = slab * (QK_SCALE * LOG2E)
                slab = slab.astype(proj_ref.dtype)
                proj_ref[rows, lo + s * LANES:lo + (s + 1) * LANES] = slab
                if c in FOX_QK_GROUPS:
                    sqnorms.append(_max_head_sqnorms(slab.astype(jnp.float32)))
        gate_ref[rows, :] = jnp.dot(xn, wgate_ref[...],
                                    preferred_element_type=jnp.float32)
        n0 = (r // PROJ_SUB) * NSQ_ROWS
        nsq_ref[n0:n0 + NSQ_ROWS, :] = jnp.concatenate(sqnorms, axis=0)


def _in_proj(x2d, pre_g, w_in, w_gate, rope, seq):
    t = x2d.shape[0]
    tiles_per_seq = seq // PROJ_TM
    row = lambda w: pl.BlockSpec((PROJ_TM, w), lambda i: (i, 0))
    table = pl.BlockSpec((PROJ_TM, 3 * LANES),
                         lambda i: (i % tiles_per_seq, 0))
    nsq_rows = PROJ_TM // PROJ_SUB * NSQ_ROWS
    return pl.pallas_call(
        _in_proj_kernel,
        grid=(t // PROJ_TM,),
        in_specs=[row(D_MODEL), _const_spec((1, D_MODEL)),
                  _const_spec(w_in.shape),
                  _const_spec((D_MODEL, LANES)), table],
        scratch_shapes=[pltpu.VMEM((D_MODEL, PROJ_WIDTH), jnp.bfloat16)],
        out_specs=[row(PROJ_WIDTH), row(LANES),
                   pl.BlockSpec((nsq_rows, LANES), lambda i: (i, 0))],
        out_shape=[jax.ShapeDtypeStruct((t, PROJ_WIDTH), jnp.bfloat16),
                   jax.ShapeDtypeStruct((t, LANES), jnp.float32),
                   jax.ShapeDtypeStruct((t // PROJ_TM * nsq_rows, LANES),
                                        jnp.float32)],
        compiler_params=pltpu.CompilerParams(
            dimension_semantics=("arbitrary",),
            vmem_limit_bytes=VMEM_LIMIT_BYTES),
        name="in_proj",
    )(x2d, pre_g, w_in, w_gate, rope)


C_PIECES = 3
C_LANES = N_FOX_HEADS * C_PIECES


def _decay_kernel(gate_ref, b_ref, piece_ref, ccol_ref, crow_ref, cpiece_ref):
    z = gate_ref[0] + b_ref[...]
    c = jnp.minimum(z, 0.0) - jnp.log1p(jnp.exp(-jnp.abs(z)))
    seq = c.shape[0]
    rows = lax.broadcasted_iota(jnp.int32, c.shape, 0)
    shift = 1
    while shift < seq:
        c = c + jnp.where(rows >= shift, pltpu.roll(c, shift, 0), 0.0)
        shift *= 2
    c = c * LOG2E
    ccol_ref[0] = c
    crow_ref[0] = c.T[:C_LANES, :]
    hi = c.astype(jnp.bfloat16)
    rest = c - hi.astype(jnp.float32)
    mid = rest.astype(jnp.bfloat16)
    lo = (rest - mid.astype(jnp.float32)).astype(jnp.bfloat16)
    piece = piece_ref[...]
    cpiece_ref[0] = jnp.where(piece == 0, hi, jnp.where(piece == 1, mid, lo))


def _decay(gate, bias_row, piece_row, batch, seq):
    gate3 = gate.reshape(batch, seq, LANES)
    col = pl.BlockSpec((1, seq, LANES), lambda b: (b, 0, 0))
    row = pl.BlockSpec((1, C_LANES, seq), lambda b: (b, 0, 0))
    return pl.pallas_call(
        _decay_kernel,
        grid=(batch,),
        in_specs=[col, _const_spec((1, LANES)), _const_spec((1, LANES))],
        out_specs=[col, row, col],
        out_shape=[jax.ShapeDtypeStruct((batch, seq, LANES), jnp.float32),
                   jax.ShapeDtypeStruct((batch, C_LANES, seq), jnp.float32),
                   jax.ShapeDtypeStruct((batch, seq, LANES), jnp.bfloat16)],
        compiler_params=pltpu.CompilerParams(
            dimension_semantics=("arbitrary",),
            vmem_limit_bytes=VMEM_LIMIT_BYTES),
        name="decay",
    )(gate3, bias_row, piece_row)


ATT_TQ = 512
ATT_HALF = ATT_TQ // 2
ATT_TK = ATT_HALF
ONES_ROWS = 16
VT_ROWS = PAIR + ONES_ROWS
SKIP_LOG2 = 160.0
BOUND_SLACK = 1.02


def _logit_bound(nsq_ref):
    n = nsq_ref[0:NSQ_ROWS, :]
    for r in range(NSQ_ROWS, nsq_ref.shape[0], NSQ_ROWS):
        n = jnp.maximum(n, nsq_ref[r:r + NSQ_ROWS, :])
    bound = jnp.sqrt(n[:UNITS] * n[UNITS:]) * BOUND_SLACK
    lane = lax.broadcasted_iota(jnp.int32, (1, LANES), 1)
    out = jnp.zeros((1, LANES), jnp.float32)
    for h in range(N_FOX_HEADS):
        value = jnp.sum(jnp.where(lane == h % 2, bound[h // 2:h // 2 + 1], 0.0),
                        axis=1, keepdims=True)
        mine = (lane >= C_PIECES * h) & (lane < C_PIECES * (h + 1))
        out = jnp.where(mine, value, out)
    return out


def _first_needed_block(ccol_ref, bh_scr, q0, tk, limit):
    n_blocks = ccol_ref.shape[1] // tk
    c_end = ccol_ref[0, pl.ds(tk - 1, n_blocks, stride=tk), :]
    c_q = ccol_ref[0, pl.ds(q0, 1), :]
    bound = 2.0 * bh_scr[...] + (c_q - c_end)
    block = lax.broadcasted_iota(jnp.int32, bound.shape, 0)
    lane = lax.broadcasted_iota(jnp.int32, bound.shape, 1)
    needed = (lane < C_LANES) & jnp.logical_not(bound < -SKIP_LOG2)
    first = jnp.min(jnp.where(needed, block, n_blocks).astype(jnp.float32))
    return jnp.minimum(first.astype(jnp.int32), limit)


def _decay_selector(u, width, th):
    row = lax.broadcasted_iota(jnp.int32, (LANES, width), 0)
    col = lax.broadcasted_iota(jnp.int32, (LANES, width), 1)
    second_map = (col & th) != 0
    first = C_PIECES * 2 * u
    owner_a = (row >= first) & (row < first + C_PIECES)
    owner_b = (row >= first + C_PIECES) & (row < first + 2 * C_PIECES)
    hit = (second_map & owner_b) | (jnp.logical_not(second_map) & owner_a)
    return jnp.where(hit, -1.0, 0.0).astype(jnp.bfloat16)


def _attn_body(is_fox, q_ref, k_ref, v_ref, ccol_ref, crow_ref, cpiece_ref,
               nsq_ref, lam_ref, g_ref, o_ref, m_scr, l_scr, acc_scr, qt_scr,
               s0_scr, s1_scr, vt_scr, bh_scr):
    tq, th, tk = ATT_TQ, ATT_HALF, ATT_TK
    qi = pl.program_id(1)
    q0 = pl.multiple_of(qi * tq, tq)

    @pl.when(qi == 0)
    def _():
        seq = v_ref.shape[0]
        for u in range(UNITS):
            vt_scr[u, :PAIR, :] = v_ref[:, u * PAIR:(u + 1) * PAIR].T
            vt_scr[u, PAIR:, :] = jnp.ones((ONES_ROWS, seq), vt_scr.dtype)
            if is_fox:
                qt_scr[u, PAIR:, :] = _decay_selector(u, 2 * tq, th)
        if is_fox:
            bh_scr[...] = _logit_bound(nsq_ref)

    first_diag = 2 * qi
    j0 = (_first_needed_block(ccol_ref, bh_scr, q0, tk, first_diag)
          if is_fox else 0)
    n_full = first_diag - j0

    lane = lax.broadcasted_iota(jnp.int32, (th, LANES), 1)
    ct = []
    for u in range(UNITS):
        pieces, ct_u = [], []
        for h in range(2):
            q = q_ref[h * th:(h + 1) * th, u * PAIR:(u + 1) * PAIR]
            zero = jnp.zeros_like(q)
            pieces += [jnp.where(lane < HEAD_DIM, q, zero),
                       jnp.where(lane >= HEAD_DIM, q, zero)]
            if is_fox:
                t0 = pl.multiple_of(q0 + h * th, th)
                ct_u += [crow_ref[0, pl.ds(C_PIECES * (2 * u + m), 1),
                                  pl.ds(t0, th)] for m in range(2)]
        qt_scr[u, :PAIR, :] = jnp.concatenate(pieces, axis=0).T
        if is_fox:
            ct.append(jnp.concatenate(ct_u, axis=1))

    m_scr[...] = jnp.full_like(m_scr, MASK_VALUE)
    l_scr[...] = jnp.zeros_like(l_scr)
    acc_scr[...] = jnp.zeros_like(acc_scr)

    def logits_to(slot, j, u, lo=0):
        k0 = pl.multiple_of(j * tk, tk)
        kb = k_ref[pl.ds(k0, tk), u * PAIR:(u + 1) * PAIR]
        if is_fox:
            kb = jnp.concatenate([kb, cpiece_ref[0, pl.ds(k0, tk), :]], axis=1)
        slot[u, :, lo:] = jnp.dot(kb, qt_scr[u, :, lo:],
                                  preferred_element_type=jnp.float32)

    def softmax_pv(slot, j, u, lo, masked):
        k0 = pl.multiple_of(j * tk, tk)
        s = slot[u, :, lo:]
        width = s.shape[1]
        if masked:
            r = lax.broadcasted_iota(jnp.int32, (tk, width), 0)
            c = lax.broadcasted_iota(jnp.int32, (tk, width), 1)
            keep = r <= (c & (th - 1))
            if lo == 0:
                keep = keep | (c >= tq)
            s = jnp.where(keep, s, MASK_VALUE)
        m_prev = m_scr[u, :, lo:]
        col_max = jnp.max(s, axis=0, keepdims=True)
        if is_fox:
            m_next = jnp.maximum(m_prev, col_max + ct[u][:, lo:])
            offset = m_next - ct[u][:, lo:]
        else:
            m_next = jnp.maximum(m_prev, col_max)
            offset = m_next
        p = jnp.exp2(s - offset).astype(vt_scr.dtype)
        alpha = jnp.exp2(m_prev - m_next)
        vt = vt_scr[u, :, pl.ds(k0, tk)]
        pv = jnp.dot(vt, p, preferred_element_type=jnp.float32)
        l_scr[u, :, lo:] = alpha * l_scr[u, :, lo:] + pv[PAIR:PAIR + 1, :]
        acc_scr[u, :, lo:] = alpha * acc_scr[u, :, lo:] + pv[:PAIR, :]
        m_scr[u, :, lo:] = m_next

    def step(j, cur, nxt, lo=0, masked=False, nxt_lo=0):
        ahead = 1
        if nxt is not None:
            for u in range(ahead):
                logits_to(nxt, j + 1, u, nxt_lo)
        for u in range(UNITS):
            softmax_pv(cur, j, u, lo, masked)
            if nxt is not None and u + ahead < UNITS:
                logits_to(nxt, j + 1, u + ahead, nxt_lo)

    for u in range(UNITS):
        logits_to(s0_scr, j0, u)

    def pair(i, carry):
        step(j0 + 2 * i, s0_scr, s1_scr)
        step(j0 + 2 * i + 1, s1_scr, s0_scr)
        return carry

    lax.fori_loop(0, lax.shift_right_logical(n_full, 1), pair, 0)

    def diagonal(cur, other):
        step(first_diag, cur, other, masked=True, nxt_lo=tq)
        step(first_diag + 1, other, None, lo=tq, masked=True)

    if is_fox:
        @pl.when((n_full & 1) == 0)
        def _():
            diagonal(s0_scr, s1_scr)

        @pl.when((n_full & 1) == 1)
        def _():
            step(first_diag - 1, s0_scr, s1_scr)
            diagonal(s1_scr, s0_scr)
    else:
        diagonal(s0_scr, s1_scr)

    if not is_fox:
        lam_rows = lam_ref[...]
        dot = lambda i: jnp.sum(lam_rows[i:i + 1] * lam_rows[i + 1:i + 2],
                                axis=1, keepdims=True)
        lam = jnp.exp(dot(0)) - jnp.exp(dot(2)) + LAMBDA_INIT
    for u in range(UNITS):
        o_t = acc_scr[u] / l_scr[u]
        for h in range(2):
            a = o_t[:, 2 * h * th:(2 * h + 1) * th]
            b = o_t[:, (2 * h + 1) * th:(2 * h + 2) * th]
            if is_fox:
                out = jnp.concatenate([a[:HEAD_DIM], b[HEAD_DIM:]], axis=0).T
            else:
                d_t = a - lam * b
                inv = lax.rsqrt(jnp.mean(d_t * d_t, axis=0, keepdims=True)
                                + RMS_EPS)
                out = (d_t * inv).T * g_ref[...] * (1.0 - LAMBDA_INIT)
            o_ref[h * th:(h + 1) * th, u * PAIR:(u + 1) * PAIR] = out.astype(
                o_ref.dtype)


def _fox_kernel(q_ref, k_ref, v_ref, ccol_ref, crow_ref, cpiece_ref, nsq_ref,
                o_ref, *scratch):
    _attn_body(True, q_ref, k_ref, v_ref, ccol_ref, crow_ref, cpiece_ref,
               nsq_ref, None, None, o_ref, *scratch)


def _diff_kernel(q_ref, k_ref, v_ref, lam_ref, g_ref, o_ref, *scratch):
    _attn_body(False, q_ref, k_ref, v_ref, None, None, None, None, lam_ref,
               g_ref, o_ref, *scratch, None)


def _attention(proj, batch, seq, group, kernel, extra_specs, extra_args,
               extra_scratch, name):
    nq = seq // ATT_TQ
    q_spec = pl.BlockSpec((ATT_TQ, GROUP), lambda b, i: (b * nq + i, group))
    kv_spec = lambda g: pl.BlockSpec((seq, GROUP), lambda b, i: (b, g))
    out_spec = pl.BlockSpec((ATT_TQ, GROUP), lambda b, i: (b * nq + i, 0))
    stat = pltpu.VMEM((UNITS, 1, 2 * ATT_TQ), jnp.float32)
    acc = pltpu.VMEM((UNITS, PAIR, 2 * ATT_TQ), jnp.float32)
    depth = PAIR + (LANES if group == 0 else 0)
    q_t = pltpu.VMEM((UNITS, depth, 2 * ATT_TQ), jnp.bfloat16)
    logit_slot = pltpu.VMEM((UNITS, ATT_TK, 2 * ATT_TQ), jnp.float32)
    v_t = pltpu.VMEM((UNITS, VT_ROWS, seq), jnp.bfloat16)
    return pl.pallas_call(
        kernel,
        grid=(batch, nq),
        in_specs=[q_spec, kv_spec(2 + group), kv_spec(4 + group)] + extra_specs,
        out_specs=out_spec,
        out_shape=jax.ShapeDtypeStruct((batch * seq, GROUP), jnp.bfloat16),
        scratch_shapes=[stat, stat, acc, q_t, logit_slot, logit_slot, v_t]
        + extra_scratch,
        compiler_params=pltpu.CompilerParams(
            dimension_semantics=("arbitrary", "arbitrary"),
            vmem_limit_bytes=VMEM_LIMIT_BYTES),
        name=name,
    )(proj, proj, proj, *extra_args)


def _rope_tables(seq):
    inv_freq = ROPE_THETA ** (-jnp.arange(0, ROPE_DIM, 2, dtype=jnp.float32)
                              / ROPE_DIM)
    ang = jnp.arange(seq).astype(jnp.float32)[:, None] * inv_freq[None, :]
    cos, sin = jnp.cos(ang), jnp.sin(ang)
    ones = jnp.ones((seq, HEAD_DIM - ROPE_DIM), jnp.float32)
    zeros = jnp.zeros((seq, HEAD_DIM - ROPE_DIM), jnp.float32)
    z8 = jnp.zeros((seq, ROPE_HALF), jnp.float32)
    cos64 = [cos, cos, ones]
    up64 = [z8, sin, zeros]
    dn64 = [-sin, z8, zeros]
    return jnp.concatenate(2 * cos64 + 2 * up64 + 2 * dn64, axis=1)


def _decay_lanes(per_head):
    rep = jnp.repeat(per_head, C_PIECES, axis=-1)
    pad = [(0, 0)] * (rep.ndim - 1) + [(0, LANES - C_LANES)]
    return jnp.pad(rep, pad)


def _gate_weight(w_in):
    gate_start = 3 * FOX_WIDTH
    fgate = w_in[:, gate_start:gate_start + N_FOX_HEADS]
    return _decay_lanes(fgate).astype(jnp.bfloat16)


def kernel(x, ffn1_pre_g, ffn1_post_g, ffn1_w_gate, ffn1_w_up, ffn1_w_down, mix_pre_g, mix_post_g, w_in, fox_forget_b, diff_lambda_q1, diff_lambda_k1, diff_lambda_q2, diff_lambda_k2, diff_subln_g, w_out, ffn2_pre_g, ffn2_post_g, ffn2_w_gate, ffn2_w_up, ffn2_w_down):
    batch, seq, d = x.shape
    bf = lambda w: w.astype(jnp.bfloat16)
    x2d = x.reshape(batch * seq, d)
    rope = _rope_tables(seq)
    for l in range(ffn1_pre_g.shape[0]):
        x2d = _ffn(x2d, ffn1_pre_g[l][None], ffn1_post_g[l][None],
                   bf(ffn1_w_gate[l]), bf(ffn1_w_up[l]), bf(ffn1_w_down[l]))

        proj, gate, nsq = _in_proj(x2d, mix_pre_g[l][None], w_in[l],
                                   _gate_weight(w_in[l]), rope, seq)
        piece_row = jnp.asarray(np.arange(LANES) % C_PIECES, jnp.int32)[None]
        ccol, crow, cpiece = _decay(gate, _decay_lanes(fox_forget_b[l])[None],
                                    piece_row, batch, seq)

        per_batch_col = pl.BlockSpec((1, seq, LANES), lambda b, i: (b, 0, 0))
        fox_o = _attention(
            proj, batch, seq, 0, _fox_kernel,
            [per_batch_col,
             pl.BlockSpec((1, C_LANES, seq), lambda b, i: (b, 0, 0)),
             per_batch_col,
             pl.BlockSpec((seq // PROJ_SUB * NSQ_ROWS, LANES),
                          lambda b, i: (b, 0))],
            [ccol, crow, cpiece, nsq],
            [pltpu.VMEM((1, LANES), jnp.float32)], "fox_attn")
        lam_rows = jnp.pad(
            jnp.stack([diff_lambda_q1[l], diff_lambda_k1[l],
                       diff_lambda_q2[l], diff_lambda_k2[l]]),
            ((0, 4), (0, LANES - HEAD_DIM)))
        diff_o = _attention(
            proj, batch, seq, 1, _diff_kernel,
            [_const_spec((8, LANES)), _const_spec((1, PAIR))],
            [lam_rows, diff_subln_g[l][None]], [], "diff_attn")

        x2d = _mix_ffn(x2d, fox_o, diff_o, bf(w_out[l]), mix_post_g[l][None],
                       ffn2_pre_g[l][None], ffn2_post_g[l][None],
                       bf(ffn2_w_gate[l]), bf(ffn2_w_up[l]), bf(ffn2_w_down[l]))
    return x2d.reshape(batch, seq, d)
```

```python
import math

import jax
import jax.numpy as jnp
import numpy as np
from jax import lax
from jax.experimental import pallas as pl
from jax.experimental.pallas import tpu as pltpu

D_MODEL = 1024
N_FOX_HEADS = 8
N_DIFF_HEADS = 4
HEAD_DIM = 64
PAIR = 2 * HEAD_DIM
FOX_WIDTH = N_FOX_HEADS * HEAD_DIM
DIFF_WIDTH = N_DIFF_HEADS * PAIR
ROPE_THETA = 500000.0
ROPE_DIM = HEAD_DIM // 4
ROPE_HALF = ROPE_DIM // 2
D_FF = 2816
FFN_RESIDUAL_WEIGHT = 0.5
RMS_EPS = 1e-6
LAMBDA_INIT = 0.8 - 0.6 * math.exp(-0.3 * 0)
QK_SCALE = HEAD_DIM ** -0.5
LOG2E = math.log2(math.e)

LANES = 128
MASK_VALUE = -1e30
VMEM_LIMIT_BYTES = 56 * 1024 * 1024

GROUP = FOX_WIDTH
PROJ_WIDTH = 6 * GROUP
UNITS = GROUP // PAIR


def _rms(x, g):
    return x * lax.rsqrt(jnp.mean(x * x, axis=-1, keepdims=True) + RMS_EPS) * g


def _const_spec(shape):
    return pl.BlockSpec(shape, lambda *_: (0,) * len(shape),
                        pipeline_mode=pl.Buffered(1))


FFN1_TM = 1024
FFN_TM = 1024
FFN_SUB = 256
FFN_CHUNKS = ((0, 1536), (1536, 1280))


def _swiglu_half_step(x, pre_g, post_g, wg_ref, wu_ref, wd_ref):
    xn = _rms(x, pre_g).astype(jnp.bfloat16)
    acc = None
    for start, width in FFN_CHUNKS:
        g = jnp.dot(xn, wg_ref[:, start:start + width],
                    preferred_element_type=jnp.float32)
        u = jnp.dot(xn, wu_ref[:, start:start + width],
                    preferred_element_type=jnp.float32)
        h = (g * jax.nn.sigmoid(g) * u).astype(jnp.bfloat16)
        part = jnp.dot(h, wd_ref[start:start + width, :],
                       preferred_element_type=jnp.float32)
        acc = part if acc is None else acc + part
    return x + FFN_RESIDUAL_WEIGHT * _rms(acc, post_g)


def _ffn_kernel(x_ref, pre_g_ref, post_g_ref, wg_ref, wu_ref, wd_ref, o_ref):
    for r in range(0, FFN1_TM, FFN_SUB):
        o_ref[r:r + FFN_SUB, :] = _swiglu_half_step(
            x_ref[r:r + FFN_SUB, :], pre_g_ref[...], post_g_ref[...],
            wg_ref, wu_ref, wd_ref)


def _mix_ffn_kernel(x_ref, fox_ref, diff_ref, wout_ref, mix_g_ref,
                    pre_g_ref, post_g_ref, wg_ref, wu_ref, wd_ref, o_ref):
    def mixed(r):
        rows = slice(r, r + FFN_SUB)
        m = jnp.dot(fox_ref[rows, :], wout_ref[:FOX_WIDTH, :],
                    preferred_element_type=jnp.float32)
        m = m + jnp.dot(diff_ref[rows, :], wout_ref[FOX_WIDTH:, :],
                        preferred_element_type=jnp.float32)
        return x_ref[rows, :] + _rms(m, mix_g_ref[...])

    x1_next = mixed(0)
    for r in range(0, FFN_TM, FFN_SUB):
        x1 = x1_next
        if r + FFN_SUB < FFN_TM:
            x1_next = mixed(r + FFN_SUB)
        o_ref[r:r + FFN_SUB, :] = _swiglu_half_step(
            x1, pre_g_ref[...], post_g_ref[...], wg_ref, wu_ref, wd_ref)


def _ffn_weight_specs():
    return [_const_spec((1, D_MODEL)), _const_spec((1, D_MODEL)),
            _const_spec((D_MODEL, D_FF)), _const_spec((D_MODEL, D_FF)),
            _const_spec((D_FF, D_MODEL))]


def _ffn(x2d, pre_g, post_g, wg, wu, wd):
    t = x2d.shape[0]
    row = pl.BlockSpec((FFN1_TM, D_MODEL), lambda i: (i, 0))
    return pl.pallas_call(
        _ffn_kernel,
        grid=(t // FFN1_TM,),
        in_specs=[row] + _ffn_weight_specs(),
        out_specs=row,
        out_shape=jax.ShapeDtypeStruct(x2d.shape, x2d.dtype),
        compiler_params=pltpu.CompilerParams(
            dimension_semantics=("arbitrary",),
            vmem_limit_bytes=VMEM_LIMIT_BYTES),
        name="ffn",
    )(x2d, pre_g, post_g, wg, wu, wd)


def _mix_ffn(x2d, fox_o, diff_o, w_out, mix_g, pre_g, post_g, wg, wu, wd):
    t = x2d.shape[0]
    row = lambda w: pl.BlockSpec((FFN_TM, w), lambda i: (i, 0))
    return pl.pallas_call(
        _mix_ffn_kernel,
        grid=(t // FFN_TM,),
        in_specs=[row(D_MODEL), row(FOX_WIDTH), row(DIFF_WIDTH),
                  _const_spec((FOX_WIDTH + DIFF_WIDTH, D_MODEL)),
                  _const_spec((1, D_MODEL))] + _ffn_weight_specs(),
        out_specs=row(D_MODEL),
        out_shape=jax.ShapeDtypeStruct(x2d.shape, x2d.dtype),
        compiler_params=pltpu.CompilerParams(
            dimension_semantics=("arbitrary",),
            vmem_limit_bytes=VMEM_LIMIT_BYTES),
        name="mix_ffn",
    )(x2d, fox_o, diff_o, w_out, mix_g, pre_g, post_g, wg, wu, wd)


PROJ_TM = 1024
PROJ_SUB = 256
ROPE_GROUPS = (1, 3)
Q_GROUPS = (0, 1)
FOX_QK_GROUPS = (0, 2)
_DIFF_START = 3 * FOX_WIDTH + N_FOX_HEADS
W_IN_GROUP_STARTS = (0, _DIFF_START, FOX_WIDTH, _DIFF_START + DIFF_WIDTH,
                     2 * FOX_WIDTH, _DIFF_START + 2 * DIFF_WIDTH)
W_CAST_ROWS = 128
NSQ_ROWS = 2 * UNITS


def _rope_slab(y, cos, sin_up, sin_dn):
    up = pltpu.roll(y, ROPE_HALF, 1)
    dn = pltpu.roll(y, LANES - ROPE_HALF, 1)
    return y * cos + up * sin_up + dn * sin_dn


def _max_head_sqnorms(slab):
    lane = lax.broadcasted_iota(jnp.int32, slab.shape, 1)
    sq = slab * slab
    head = lambda keep: jnp.max(
        jnp.sum(jnp.where(keep, sq, 0.0), axis=1, keepdims=True),
        axis=0, keepdims=True)
    lane1 = lax.broadcasted_iota(jnp.int32, (1, LANES), 1)
    return jnp.where(lane1 == 0, head(lane < HEAD_DIM),
                     jnp.where(lane1 == 1, head(lane >= HEAD_DIM), 0.0))


def _in_proj_kernel(x_ref, g_ref, win_ref, wgate_ref, cos_ref, sup_ref, sdn_ref,
                    proj_ref, gate_ref, nsq_ref, w_scr):
    @pl.when(pl.program_id(0) == 0)
    def _():
        for dst, src in enumerate(W_IN_GROUP_STARTS):
            for r in range(0, D_MODEL, W_CAST_ROWS):
                w_scr[r:r + W_CAST_ROWS, dst * GROUP:(dst + 1) * GROUP] = (
                    win_ref[0, r:r + W_CAST_ROWS, src:src + GROUP].astype(
                        w_scr.dtype))

    for r in range(0, PROJ_TM, PROJ_SUB):
        rows = slice(r, r + PROJ_SUB)
        xn = _rms(x_ref[rows, :], g_ref[...]).astype(jnp.bfloat16)
        cos, sup, sdn = cos_ref[rows, :], sup_ref[rows, :], sdn_ref[rows, :]
        sqnorms = []
        for c in range(PROJ_WIDTH // GROUP):
            lo = c * GROUP
            y = jnp.dot(xn, w_scr[:, lo:lo + GROUP],
                        preferred_element_type=jnp.float32)
            for s in range(UNITS):
                slab = y[:, s * LANES:(s + 1) * LANES]
                if c in ROPE_GROUPS:
                    slab = _rope_slab(slab, cos, sup, sdn)
                if c in Q_GROUPS:
                    slab = slab * (QK_SCALE * LOG2E)
                slab = slab.astype(proj_ref.dtype)
                proj_ref[rows, lo + s * LANES:lo + (s + 1) * LANES] = slab
                if c in FOX_QK_GROUPS:
                    sqnorms.append(_max_head_sqnorms(slab.astype(jnp.float32)))
        gate_ref[rows, :] = jnp.dot(xn, wgate_ref[...],
                                    preferred_element_type=jnp.float32)
        n0 = (r // PROJ_SUB) * NSQ_ROWS
        nsq_ref[n0:n0 + NSQ_ROWS, :] = jnp.concatenate(sqnorms, axis=0)


def _in_proj(x2d, pre_g, w_in, layer, w_gate, rope, seq):
    t = x2d.shape[0]
    tiles_per_seq = seq // PROJ_TM
    row = lambda w: pl.BlockSpec((PROJ_TM, w), lambda i: (i, 0))
    table = pl.BlockSpec((PROJ_TM, LANES), lambda i: (i % tiles_per_seq, 0))
    w_spec = pl.BlockSpec((1,) + w_in.shape[1:], lambda i: (layer, 0, 0),
                          pipeline_mode=pl.Buffered(1))
    nsq_rows = PROJ_TM // PROJ_SUB * NSQ_ROWS
    return pl.pallas_call(
        _in_proj_kernel,
        grid=(t // PROJ_TM,),
        in_specs=[row(D_MODEL), _const_spec((1, D_MODEL)), w_spec,
                  _const_spec((D_MODEL, LANES)), table, table, table],
        scratch_shapes=[pltpu.VMEM((D_MODEL, PROJ_WIDTH), jnp.bfloat16)],
        out_specs=[row(PROJ_WIDTH), row(LANES),
                   pl.BlockSpec((nsq_rows, LANES), lambda i: (i, 0))],
        out_shape=[jax.ShapeDtypeStruct((t, PROJ_WIDTH), jnp.bfloat16),
                   jax.ShapeDtypeStruct((t, LANES), jnp.float32),
                   jax.ShapeDtypeStruct((t // PROJ_TM * nsq_rows, LANES),
                                        jnp.float32)],
        compiler_params=pltpu.CompilerParams(
            dimension_semantics=("arbitrary",),
            vmem_limit_bytes=VMEM_LIMIT_BYTES),
        name="in_proj",
    )(x2d, pre_g, w_in, w_gate, *rope)


C_PIECES = 3
C_LANES = N_FOX_HEADS * C_PIECES


def _decay_kernel(gate_ref, b_ref, piece_ref, ccol_ref, crow_ref, cpiece_ref):
    z = gate_ref[0] + b_ref[...]
    c = jnp.minimum(z, 0.0) - jnp.log1p(jnp.exp(-jnp.abs(z)))
    seq = c.shape[0]
    rows = lax.broadcasted_iota(jnp.int32, c.shape, 0)
    shift = 1
    while shift < seq:
        c = c + jnp.where(rows >= shift, pltpu.roll(c, shift, 0), 0.0)
        shift *= 2
    c = c * LOG2E
    ccol_ref[0] = c
    crow_ref[0] = c.T[:C_LANES, :]
    hi = c.astype(jnp.bfloat16)
    rest = c - hi.astype(jnp.float32)
    mid = rest.astype(jnp.bfloat16)
    lo = (rest - mid.astype(jnp.float32)).astype(jnp.bfloat16)
    piece = piece_ref[...]
    cpiece_ref[0] = jnp.where(piece == 0, hi, jnp.where(piece == 1, mid, lo))


def _decay(gate, bias_row, piece_row, batch, seq):
    gate3 = gate.reshape(batch, seq, LANES)
    col = pl.BlockSpec((1, seq, LANES), lambda b: (b, 0, 0))
    row = pl.BlockSpec((1, C_LANES, seq), lambda b: (b, 0, 0))
    return pl.pallas_call(
        _decay_kernel,
        grid=(batch,),
        in_specs=[col, _const_spec((1, LANES)), _const_spec((1, LANES))],
        out_specs=[col, row, col],
        out_shape=[jax.ShapeDtypeStruct((batch, seq, LANES), jnp.float32),
                   jax.ShapeDtypeStruct((batch, C_LANES, seq), jnp.float32),
                   jax.ShapeDtypeStruct((batch, seq, LANES), jnp.bfloat16)],
        compiler_params=pltpu.CompilerParams(
            dimension_semantics=("arbitrary",),
            vmem_limit_bytes=VMEM_LIMIT_BYTES),
        name="decay",
    )(gate3, bias_row, piece_row)


ATT_TQ = 512
ATT_HALF = ATT_TQ // 2
ATT_TK = ATT_HALF
ONES_ROWS = 16
VT_ROWS = PAIR + ONES_ROWS
SKIP_LOG2 = 160.0
BOUND_SLACK = 1.02


def _logit_bound(nsq_ref):
    n = nsq_ref[0:NSQ_ROWS, :]
    for r in range(NSQ_ROWS, nsq_ref.shape[0], NSQ_ROWS):
        n = jnp.maximum(n, nsq_ref[r:r + NSQ_ROWS, :])
    bound = jnp.sqrt(n[:UNITS] * n[UNITS:]) * BOUND_SLACK
    lane = lax.broadcasted_iota(jnp.int32, (1, LANES), 1)
    out = jnp.zeros((1, LANES), jnp.float32)
    for h in range(N_FOX_HEADS):
        value = jnp.sum(jnp.where(lane == h % 2, bound[h // 2:h // 2 + 1], 0.0),
                        axis=1, keepdims=True)
        mine = (lane >= C_PIECES * h) & (lane < C_PIECES * (h + 1))
        out = jnp.where(mine, value, out)
    return out


def _first_needed_block(ccol_ref, bh_scr, q0, tk, limit):
    n_blocks = ccol_ref.shape[1] // tk
    c_end = ccol_ref[0, pl.ds(tk - 1, n_blocks, stride=tk), :]
    c_q = ccol_ref[0, pl.ds(q0, 1), :]
    bound = 2.0 * bh_scr[...] + (c_q - c_end)
    block = lax.broadcasted_iota(jnp.int32, bound.shape, 0)
    lane = lax.broadcasted_iota(jnp.int32, bound.shape, 1)
    needed = (lane < C_LANES) & jnp.logical_not(bound < -SKIP_LOG2)
    first = jnp.min(jnp.where(needed, block, n_blocks).astype(jnp.float32))
    return jnp.minimum(first.astype(jnp.int32), limit)


def _decay_selector(u, width, th):
    row = lax.broadcasted_iota(jnp.int32, (LANES, width), 0)
    col = lax.broadcasted_iota(jnp.int32, (LANES, width), 1)
    second_map = (col & th) != 0
    first = C_PIECES * 2 * u
    owner_a = (row >= first) & (row < first + C_PIECES)
    owner_b = (row >= first + C_PIECES) & (row < first + 2 * C_PIECES)
    hit = (second_map & owner_b) | (jnp.logical_not(second_map) & owner_a)
    return jnp.where(hit, -1.0, 0.0).astype(jnp.bfloat16)


def _attn_body(is_fox, q_ref, k_ref, v_ref, ccol_ref, crow_ref, cpiece_ref,
               nsq_ref, lam_ref, g_ref, o_ref, m_scr, l_scr, acc_scr, qt_scr,
               s0_scr, s1_scr, vt_scr, bh_scr):
    tq, th, tk = ATT_TQ, ATT_HALF, ATT_TK
    qi = pl.program_id(1)
    q0 = pl.multiple_of(qi * tq, tq)

    @pl.when(qi == 0)
    def _():
        seq = v_ref.shape[0]
        for u in range(UNITS):
            vt_scr[u, :PAIR, :] = v_ref[:, u * PAIR:(u + 1) * PAIR].T
            vt_scr[u, PAIR:, :] = jnp.ones((ONES_ROWS, seq), vt_scr.dtype)
            if is_fox:
                qt_scr[u, PAIR:, :] = _decay_selector(u, 2 * tq, th)
        if is_fox:
            bh_scr[...] = _logit_bound(nsq_ref)

    first_diag = 2 * qi
    j0 = (_first_needed_block(ccol_ref, bh_scr, q0, tk, first_diag)
          if is_fox else 0)
    n_full = first_diag - j0

    lane = lax.broadcasted_iota(jnp.int32, (th, LANES), 1)
    ct = []
    for u in range(UNITS):
        pieces, ct_u = [], []
        for h in range(2):
            q = q_ref[h * th:(h + 1) * th, u * PAIR:(u + 1) * PAIR]
            zero = jnp.zeros_like(q)
            pieces += [jnp.where(lane < HEAD_DIM, q, zero),
                       jnp.where(lane >= HEAD_DIM, q, zero)]
            if is_fox:
                t0 = pl.multiple_of(q0 + h * th, th)
                ct_u += [crow_ref[0, pl.ds(C_PIECES * (2 * u + m), 1),
                                  pl.ds(t0, th)] for m in range(2)]
        qt_scr[u, :PAIR, :] = jnp.concatenate(pieces, axis=0).T
        if is_fox:
            ct.append(jnp.concatenate(ct_u, axis=1))

    m_scr[...] = jnp.full_like(m_scr, MASK_VALUE)
    l_scr[...] = jnp.zeros_like(l_scr)
    acc_scr[...] = jnp.zeros_like(acc_scr)

    def logits_to(slot, j, u, lo=0):
        k0 = pl.multiple_of(j * tk, tk)
        kb = k_ref[pl.ds(k0, tk), u * PAIR:(u + 1) * PAIR]
        if is_fox:
            kb = jnp.concatenate([kb, cpiece_ref[0, pl.ds(k0, tk), :]], axis=1)
        slot[u, :, lo:] = jnp.dot(kb, qt_scr[u, :, lo:],
                                  preferred_element_type=jnp.float32)

    def softmax_pv(slot, j, u, lo, masked):
        k0 = pl.multiple_of(j * tk, tk)
        s = slot[u, :, lo:]
        width = s.shape[1]
        if masked:
            r = lax.broadcasted_iota(jnp.int32, (tk, width), 0)
            c = lax.broadcasted_iota(jnp.int32, (tk, width), 1)
            keep = r <= (c & (th - 1))
            if lo == 0:
                keep = keep | (c >= tq)
            s = jnp.where(keep, s, MASK_VALUE)
        m_prev = m_scr[u, :, lo:]
        col_max = jnp.max(s, axis=0, keepdims=True)
        if is_fox:
            m_next = jnp.maximum(m_prev, col_max + ct[u][:, lo:])
            offset = m_next - ct[u][:, lo:]
        else:
            m_next = jnp.maximum(m_prev, col_max)
            offset = m_next
        p = jnp.exp2(s - offset).astype(vt_scr.dtype)
        alpha = jnp.exp2(m_prev - m_next)
        vt = vt_scr[u, :, pl.ds(k0, tk)]
        pv = jnp.dot(vt, p, preferred_element_type=jnp.float32)
        l_scr[u, :, lo:] = alpha * l_scr[u, :, lo:] + pv[PAIR:PAIR + 1, :]
        acc_scr[u, :, lo:] = alpha * acc_scr[u, :, lo:] + pv[:PAIR, :]
        m_scr[u, :, lo:] = m_next

    def step(j, cur, nxt, lo=0, masked=False, nxt_lo=0):
        ahead = 1
        if nxt is not None:
            for u in range(ahead):
                logits_to(nxt, j + 1, u, nxt_lo)
        for u in range(UNITS):
            softmax_pv(cur, j, u, lo, masked)
            if nxt is not None and u + ahead < UNITS:
                logits_to(nxt, j + 1, u + ahead, nxt_lo)

    for u in range(UNITS):
        logits_to(s0_scr, j0, u)

    def pair(i, carry):
        step(j0 + 2 * i, s0_scr, s1_scr)
        step(j0 + 2 * i + 1, s1_scr, s0_scr)
        return carry

    lax.fori_loop(0, lax.shift_right_logical(n_full, 1), pair, 0)

    def diagonal(cur, other):
        step(first_diag, cur, other, masked=True, nxt_lo=tq)
        step(first_diag + 1, other, None, lo=tq, masked=True)

    if is_fox:
        @pl.when((n_full & 1) == 0)
        def _():
            diagonal(s0_scr, s1_scr)

        @pl.when((n_full & 1) == 1)
        def _():
            step(first_diag - 1, s0_scr, s1_scr)
            diagonal(s1_scr, s0_scr)
    else:
        diagonal(s0_scr, s1_scr)

    if not is_fox:
        lam_rows = lam_ref[...]
        dot = lambda i: jnp.sum(lam_rows[i:i + 1] * lam_rows[i + 1:i + 2],
                                axis=1, keepdims=True)
        lam = jnp.exp(dot(0)) - jnp.exp(dot(2)) + LAMBDA_INIT
    for u in range(UNITS):
        o_t = acc_scr[u] / l_scr[u]
        for h in range(2):
            a = o_t[:, 2 * h * th:(2 * h + 1) * th]
            b = o_t[:, (2 * h + 1) * th:(2 * h + 2) * th]
            if is_fox:
                out = jnp.concatenate([a[:HEAD_DIM], b[HEAD_DIM:]], axis=0).T
            else:
                d_t = a - lam * b
                inv = lax.rsqrt(jnp.mean(d_t * d_t, axis=0, keepdims=True)
                                + RMS_EPS)
                out = (d_t * inv).T * g_ref[...] * (1.0 - LAMBDA_INIT)
            o_ref[h * th:(h + 1) * th, u * PAIR:(u + 1) * PAIR] = out.astype(
                o_ref.dtype)


def _fox_kernel(q_ref, k_ref, v_ref, ccol_ref, crow_ref, cpiece_ref, nsq_ref,
                o_ref, *scratch):
    _attn_body(True, q_ref, k_ref, v_ref, ccol_ref, crow_ref, cpiece_ref,
               nsq_ref, None, None, o_ref, *scratch)


def _diff_kernel(q_ref, k_ref, v_ref, lam_ref, g_ref, o_ref, *scratch):
    _attn_body(False, q_ref, k_ref, v_ref, None, None, None, None, lam_ref,
               g_ref, o_ref, *scratch, None)


def _attention(proj, batch, seq, group, kernel, extra_specs, extra_args,
               extra_scratch, name):
    nq = seq // ATT_TQ
    q_spec = pl.BlockSpec((ATT_TQ, GROUP), lambda b, i: (b * nq + i, group))
    kv_spec = lambda g: pl.BlockSpec((seq, GROUP), lambda b, i: (b, g))
    out_spec = pl.BlockSpec((ATT_TQ, GROUP), lambda b, i: (b * nq + i, 0))
    stat = pltpu.VMEM((UNITS, 1, 2 * ATT_TQ), jnp.float32)
    acc = pltpu.VMEM((UNITS, PAIR, 2 * ATT_TQ), jnp.float32)
    depth = PAIR + (LANES if group == 0 else 0)
    q_t = pltpu.VMEM((UNITS, depth, 2 * ATT_TQ), jnp.bfloat16)
    logit_slot = pltpu.VMEM((UNITS, ATT_TK, 2 * ATT_TQ), jnp.float32)
    v_t = pltpu.VMEM((UNITS, VT_ROWS, seq), jnp.bfloat16)
    return pl.pallas_call(
        kernel,
        grid=(batch, nq),
        in_specs=[q_spec, kv_spec(2 + group), kv_spec(4 + group)] + extra_specs,
        out_specs=out_spec,
        out_shape=jax.ShapeDtypeStruct((batch * seq, GROUP), jnp.bfloat16),
        scratch_shapes=[stat, stat, acc, q_t, logit_slot, logit_slot, v_t]
        + extra_scratch,
        compiler_params=pltpu.CompilerParams(
            dimension_semantics=("arbitrary", "arbitrary"),
            vmem_limit_bytes=VMEM_LIMIT_BYTES),
        name=name,
    )(proj, proj, proj, *extra_args)


def _rope_tables(seq):
    inv_freq = ROPE_THETA ** (-jnp.arange(0, ROPE_DIM, 2, dtype=jnp.float32)
                              / ROPE_DIM)
    p = np.arange(LANES) % HEAD_DIM
    lane_freq = jnp.where(p < ROPE_DIM, inv_freq[p % ROPE_HALF], 0.0)
    ang = jnp.arange(seq).astype(jnp.float32)[:, None] * lane_freq[None, :]
    cos, sin = jnp.cos(ang), jnp.sin(ang)
    sin_up = jnp.where((p >= ROPE_HALF) & (p < ROPE_DIM), sin, 0.0)
    sin_dn = jnp.where(p < ROPE_HALF, -sin, 0.0)
    return cos, sin_up, sin_dn


def _decay_lanes(per_head):
    rep = jnp.repeat(per_head, C_PIECES, axis=-1)
    pad = [(0, 0)] * (rep.ndim - 1) + [(0, LANES - C_LANES)]
    return jnp.pad(rep, pad)


def _gate_weight(w_in):
    gate_start = 3 * FOX_WIDTH
    fgate = w_in[:, gate_start:gate_start + N_FOX_HEADS]
    return _decay_lanes(fgate).astype(jnp.bfloat16)


def kernel(x, ffn1_pre_g, ffn1_post_g, ffn1_w_gate, ffn1_w_up, ffn1_w_down, mix_pre_g, mix_post_g, w_in, fox_forget_b, diff_lambda_q1, diff_lambda_k1, diff_lambda_q2, diff_lambda_k2, diff_subln_g, w_out, ffn2_pre_g, ffn2_post_g, ffn2_w_gate, ffn2_w_up, ffn2_w_down):
    batch, seq, d = x.shape
    bf = lambda w: w.astype(jnp.bfloat16)
    x2d = x.reshape(batch * seq, d)
    rope = _rope_tables(seq)
    for l in range(ffn1_pre_g.shape[0]):
        x2d = _ffn(x2d, ffn1_pre_g[l][None], ffn1_post_g[l][None],
                   bf(ffn1_w_gate[l]), bf(ffn1_w_up[l]), bf(ffn1_w_down[l]))

        proj, gate, nsq = _in_proj(x2d, mix_pre_g[l][None], w_in, l,
                                   _gate_weight(w_in[l]), rope, seq)
        piece_row = jnp.asarray(np.arange(LANES) % C_PIECES, jnp.int32)[None]
        ccol, crow, cpiece = _decay(gate, _decay_lanes(fox_forget_b[l])[None],
                                    piece_row, batch, seq)

        per_batch_col = pl.BlockSpec((1, seq, LANES), lambda b, i: (b, 0, 0))
        fox_o = _attention(
            proj, batch, seq, 0, _fox_kernel,
            [per_batch_col,
             pl.BlockSpec((1, C_LANES, seq), lambda b, i: (b, 0, 0)),
             per_batch_col,
             pl.BlockSpec((seq // PROJ_SUB * NSQ_ROWS, LANES),
                          lambda b, i: (b, 0))],
            [ccol, crow, cpiece, nsq],
            [pltpu.VMEM((1, LANES), jnp.float32)], "fox_attn")
        lam_rows = jnp.pad(
            jnp.stack([diff_lambda_q1[l], diff_lambda_k1[l],
                       diff_lambda_q2[l], diff_lambda_k2[l]]),
            ((0, 4), (0, LANES - HEAD_DIM)))
        diff_o = _attention(
            proj, batch, seq, 1, _diff_kernel,
            [_const_spec((8, LANES)), _const_spec((1, PAIR))],
            [lam_rows, diff_subln_g[l][None]], [], "diff_attn")

        x2d = _mix_ffn(x2d, fox_o, diff_o, bf(w_out[l]), mix_post_g[l][None],
                       ffn2_pre_g[l][None], ffn2_post_g[l][None],
                       bf(ffn2_w_gate[l]), bf(ffn2_w_up[l]), bf(ffn2_w_down[l]))
    return x2d.reshape(batch, seq, d)
```

```python
import math

import jax
import jax.numpy as jnp
import numpy as np
from jax import lax
from jax.experimental import pallas as pl
from jax.experimental.pallas import tpu as pltpu

D_MODEL = 1024
N_FOX_HEADS = 8
N_DIFF_HEADS = 4
HEAD_DIM = 64
PAIR = 2 * HEAD_DIM
FOX_WIDTH = N_FOX_HEADS * HEAD_DIM
DIFF_WIDTH = N_DIFF_HEADS * PAIR
ROPE_THETA = 500000.0
ROPE_DIM = HEAD_DIM // 4
ROPE_HALF = ROPE_DIM // 2
D_FF = 2816
FFN_RESIDUAL_WEIGHT = 0.5
RMS_EPS = 1e-6
LAMBDA_INIT = 0.8 - 0.6 * math.exp(-0.3 * 0)
QK_SCALE = HEAD_DIM ** -0.5
LOG2E = math.log2(math.e)

LANES = 128
MASK_VALUE = -1e30
VMEM_LIMIT_BYTES = 56 * 1024 * 1024

GROUP = FOX_WIDTH
PROJ_WIDTH = 6 * GROUP
UNITS = GROUP // PAIR


def _rms(x, g):
    return x * lax.rsqrt(jnp.mean(x * x, axis=-1, keepdims=True) + RMS_EPS) * g


def _const_spec(shape):
    return pl.BlockSpec(shape, lambda *_: (0,) * len(shape),
                        pipeline_mode=pl.Buffered(1))


FFN_TM = 1024
FFN_SUB = 256
FFN_CHUNKS = ((0, 1536), (1536, 1280))


def _swiglu_half_step(x, pre_g, post_g, wg_ref, wu_ref, wd_ref):
    xn = _rms(x, pre_g).astype(jnp.bfloat16)
    acc = None
    for start, width in FFN_CHUNKS:
        g = jnp.dot(xn, wg_ref[:, start:start + width],
                    preferred_element_type=jnp.float32)
        u = jnp.dot(xn, wu_ref[:, start:start + width],
                    preferred_element_type=jnp.float32)
        h = (g * jax.nn.sigmoid(g) * u).astype(jnp.bfloat16)
        part = jnp.dot(h, wd_ref[start:start + width, :],
                       preferred_element_type=jnp.float32)
        acc = part if acc is None else acc + part
    return x + FFN_RESIDUAL_WEIGHT * _rms(acc, post_g)


def _ffn_kernel(x_ref, pre_g_ref, post_g_ref, wg_ref, wu_ref, wd_ref, o_ref):
    for r in range(0, FFN_TM, FFN_SUB):
        o_ref[r:r + FFN_SUB, :] = _swiglu_half_step(
            x_ref[r:r + FFN_SUB, :], pre_g_ref[...], post_g_ref[...],
            wg_ref, wu_ref, wd_ref)


def _mix_ffn_kernel(x_ref, fox_ref, diff_ref, wout_ref, mix_g_ref,
                    pre_g_ref, post_g_ref, wg_ref, wu_ref, wd_ref, o_ref):
    def mixed(r):
        rows = slice(r, r + FFN_SUB)
        m = jnp.dot(fox_ref[rows, :], wout_ref[:FOX_WIDTH, :],
                    preferred_element_type=jnp.float32)
        m = m + jnp.dot(diff_ref[rows, :], wout_ref[FOX_WIDTH:, :],
                        preferred_element_type=jnp.float32)
        return x_ref[rows, :] + _rms(m, mix_g_ref[...])

    x1_next = mixed(0)
    for r in range(0, FFN_TM, FFN_SUB):
        x1 = x1_next
        if r + FFN_SUB < FFN_TM:
            x1_next = mixed(r + FFN_SUB)
        o_ref[r:r + FFN_SUB, :] = _swiglu_half_step(
            x1, pre_g_ref[...], post_g_ref[...], wg_ref, wu_ref, wd_ref)


def _ffn_weight_specs():
    return [_const_spec((1, D_MODEL)), _const_spec((1, D_MODEL)),
            _const_spec((D_MODEL, D_FF)), _const_spec((D_MODEL, D_FF)),
            _const_spec((D_FF, D_MODEL))]


def _ffn(x2d, pre_g, post_g, wg, wu, wd):
    t = x2d.shape[0]
    row = pl.BlockSpec((FFN_TM, D_MODEL), lambda i: (i, 0))
    return pl.pallas_call(
        _ffn_kernel,
        grid=(t // FFN_TM,),
        in_specs=[row] + _ffn_weight_specs(),
        out_specs=row,
        out_shape=jax.ShapeDtypeStruct(x2d.shape, x2d.dtype),
        compiler_params=pltpu.CompilerParams(
            dimension_semantics=("arbitrary",),
            vmem_limit_bytes=VMEM_LIMIT_BYTES),
        name="ffn",
    )(x2d, pre_g, post_g, wg, wu, wd)


def _mix_ffn(x2d, fox_o, diff_o, w_out, mix_g, pre_g, post_g, wg, wu, wd):
    t = x2d.shape[0]
    row = lambda w: pl.BlockSpec((FFN_TM, w), lambda i: (i, 0))
    return pl.pallas_call(
        _mix_ffn_kernel,
        grid=(t // FFN_TM,),
        in_specs=[row(D_MODEL), row(FOX_WIDTH), row(DIFF_WIDTH),
                  _const_spec((FOX_WIDTH + DIFF_WIDTH, D_MODEL)),
                  _const_spec((1, D_MODEL))] + _ffn_weight_specs(),
        out_specs=row(D_MODEL),
        out_shape=jax.ShapeDtypeStruct(x2d.shape, x2d.dtype),
        compiler_params=pltpu.CompilerParams(
            dimension_semantics=("arbitrary",),
            vmem_limit_bytes=VMEM_LIMIT_BYTES),
        name="mix_ffn",
    )(x2d, fox_o, diff_o, w_out, mix_g, pre_g, post_g, wg, wu, wd)


PROJ_TM = 1024
PROJ_SUB = 256
ROPE_GROUPS = (1, 3)
Q_GROUPS = (0, 1)
FOX_QK_GROUPS = (0, 2)
_DIFF_START = 3 * FOX_WIDTH + N_FOX_HEADS
W_IN_GROUP_STARTS = (0, _DIFF_START, FOX_WIDTH, _DIFF_START + DIFF_WIDTH,
                     2 * FOX_WIDTH, _DIFF_START + 2 * DIFF_WIDTH)
W_CAST_ROWS = 128
NSQ_ROWS = 2 * UNITS


def _rope_slab(y, cos, sin_up, sin_dn):
    up = pltpu.roll(y, ROPE_HALF, 1)
    dn = pltpu.roll(y, LANES - ROPE_HALF, 1)
    return y * cos + up * sin_up + dn * sin_dn


def _max_head_sqnorms(slab):
    lane = lax.broadcasted_iota(jnp.int32, slab.shape, 1)
    sq = slab * slab
    head = lambda keep: jnp.max(
        jnp.sum(jnp.where(keep, sq, 0.0), axis=1, keepdims=True),
        axis=0, keepdims=True)
    lane1 = lax.broadcasted_iota(jnp.int32, (1, LANES), 1)
    return jnp.where(lane1 == 0, head(lane < HEAD_DIM),
                     jnp.where(lane1 == 1, head(lane >= HEAD_DIM), 0.0))


def _in_proj_kernel(x_ref, g_ref, win_ref, wgate_ref, cos_ref, sup_ref, sdn_ref,
                    proj_ref, gate_ref, nsq_ref, w_scr):
    @pl.when(pl.program_id(0) == 0)
    def _():
        for dst, src in enumerate(W_IN_GROUP_STARTS):
            for r in range(0, D_MODEL, W_CAST_ROWS):
                w_scr[r:r + W_CAST_ROWS, dst * GROUP:(dst + 1) * GROUP] = (
                    win_ref[0, r:r + W_CAST_ROWS, src:src + GROUP].astype(
                        w_scr.dtype))

    for r in range(0, PROJ_TM, PROJ_SUB):
        rows = slice(r, r + PROJ_SUB)
        xn = _rms(x_ref[rows, :], g_ref[...]).astype(jnp.bfloat16)
        cos, sup, sdn = cos_ref[rows, :], sup_ref[rows, :], sdn_ref[rows, :]
        sqnorms = []
        for c in range(PROJ_WIDTH // GROUP):
            lo = c * GROUP
            y = jnp.dot(xn, w_scr[:, lo:lo + GROUP],
                        preferred_element_type=jnp.float32)
            for s in range(UNITS):
                slab = y[:, s * LANES:(s + 1) * LANES]
                if c in ROPE_GROUPS:
                    slab = _rope_slab(slab, cos, sup, sdn)
                if c in Q_GROUPS:
                    slab = slab * (QK_SCALE * LOG2E)
                slab = slab.astype(proj_ref.dtype)
                proj_ref[rows, lo + s * LANES:lo + (s + 1) * LANES] = slab
                if c in FOX_QK_GROUPS:
                    sqnorms.append(_max_head_sqnorms(slab.astype(jnp.float32)))
        gate_ref[rows, :] = jnp.dot(xn, wgate_ref[...],
                                    preferred_element_type=jnp.float32)
        n0 = (r // PROJ_SUB) * NSQ_ROWS
        nsq_ref[n0:n0 + NSQ_ROWS, :] = jnp.concatenate(sqnorms, axis=0)


def _in_proj(x2d, pre_g, w_in, layer, w_gate, rope, seq):
    t = x2d.shape[0]
    tiles_per_seq = seq // PROJ_TM
    row = lambda w: pl.BlockSpec((PROJ_TM, w), lambda i: (i, 0))
    table = pl.BlockSpec((PROJ_TM, LANES), lambda i: (i % tiles_per_seq, 0))
    w_spec = pl.BlockSpec((1,) + w_in.shape[1:], lambda i: (layer, 0, 0),
                          pipeline_mode=pl.Buffered(1))
    nsq_rows = PROJ_TM // PROJ_SUB * NSQ_ROWS
    return pl.pallas_call(
        _in_proj_kernel,
        grid=(t // PROJ_TM,),
        in_specs=[row(D_MODEL), _const_spec((1, D_MODEL)), w_spec,
                  _const_spec((D_MODEL, LANES)), table, table, table],
        scratch_shapes=[pltpu.VMEM((D_MODEL, PROJ_WIDTH), jnp.bfloat16)],
        out_specs=[row(PROJ_WIDTH), row(LANES),
                   pl.BlockSpec((nsq_rows, LANES), lambda i: (i, 0))],
        out_shape=[jax.ShapeDtypeStruct((t, PROJ_WIDTH), jnp.bfloat16),
                   jax.ShapeDtypeStruct((t, LANES), jnp.float32),
                   jax.ShapeDtypeStruct((t // PROJ_TM * nsq_rows, LANES),
                                        jnp.float32)],
        compiler_params=pltpu.CompilerParams(
            dimension_semantics=("arbitrary",),
            vmem_limit_bytes=VMEM_LIMIT_BYTES),
        name="in_proj",
    )(x2d, pre_g, w_in, w_gate, *rope)


C_PIECES = 3
C_LANES = N_FOX_HEADS * C_PIECES


def _decay_kernel(gate_ref, b_ref, piece_ref, ccol_ref, crow_ref, cpiece_ref):
    z = gate_ref[0] + b_ref[...]
    c = jnp.minimum(z, 0.0) - jnp.log1p(jnp.exp(-jnp.abs(z)))
    seq = c.shape[0]
    rows = lax.broadcasted_iota(jnp.int32, c.shape, 0)
    shift = 1
    while shift < seq:
        c = c + jnp.where(rows >= shift, pltpu.roll(c, shift, 0), 0.0)
        shift *= 2
    c = c * LOG2E
    ccol_ref[0] = c
    crow_ref[0] = c.T[:C_LANES, :]
    hi = c.astype(jnp.bfloat16)
    rest = c - hi.astype(jnp.float32)
    mid = rest.astype(jnp.bfloat16)
    lo = (rest - mid.astype(jnp.float32)).astype(jnp.bfloat16)
    piece = piece_ref[...]
    cpiece_ref[0] = jnp.where(piece == 0, hi, jnp.where(piece == 1, mid, lo))


def _decay(gate, bias_row, piece_row, batch, seq):
    gate3 = gate.reshape(batch, seq, LANES)
    col = pl.BlockSpec((1, seq, LANES), lambda b: (b, 0, 0))
    row = pl.BlockSpec((1, C_LANES, seq), lambda b: (b, 0, 0))
    return pl.pallas_call(
        _decay_kernel,
        grid=(batch,),
        in_specs=[col, _const_spec((1, LANES)), _const_spec((1, LANES))],
        out_specs=[col, row, col],
        out_shape=[jax.ShapeDtypeStruct((batch, seq, LANES), jnp.float32),
                   jax.ShapeDtypeStruct((batch, C_LANES, seq), jnp.float32),
                   jax.ShapeDtypeStruct((batch, seq, LANES), jnp.bfloat16)],
        compiler_params=pltpu.CompilerParams(
            dimension_semantics=("arbitrary",),
            vmem_limit_bytes=VMEM_LIMIT_BYTES),
        name="decay",
    )(gate3, bias_row, piece_row)


ATT_TK = 256
ATT_TQ = (2 * ATT_TK, 4 * ATT_TK)
LOGITS_LEAD = 1
ONES_ROWS = 16
VT_ROWS = PAIR + ONES_ROWS
SKIP_LOG2 = 160.0
BOUND_SLACK = 1.02


def _logit_bound(nsq_ref):
    n = nsq_ref[0:NSQ_ROWS, :]
    for r in range(NSQ_ROWS, nsq_ref.shape[0], NSQ_ROWS):
        n = jnp.maximum(n, nsq_ref[r:r + NSQ_ROWS, :])
    bound = jnp.sqrt(n[:UNITS] * n[UNITS:]) * BOUND_SLACK
    lane = lax.broadcasted_iota(jnp.int32, (1, LANES), 1)
    out = jnp.zeros((1, LANES), jnp.float32)
    for h in range(N_FOX_HEADS):
        value = jnp.sum(jnp.where(lane == h % 2, bound[h // 2:h // 2 + 1], 0.0),
                        axis=1, keepdims=True)
        mine = (lane >= C_PIECES * h) & (lane < C_PIECES * (h + 1))
        out = jnp.where(mine, value, out)
    return out


def _first_needed_block(ccol_ref, bh_scr, q0, tk, limit):
    n_blocks = ccol_ref.shape[1] // tk
    c_end = ccol_ref[0, pl.ds(tk - 1, n_blocks, stride=tk), :]
    c_q = ccol_ref[0, pl.ds(q0, 1), :]
    bound = 2.0 * bh_scr[...] + (c_q - c_end)
    block = lax.broadcasted_iota(jnp.int32, bound.shape, 0)
    lane = lax.broadcasted_iota(jnp.int32, bound.shape, 1)
    needed = (lane < C_LANES) & jnp.logical_not(bound < -SKIP_LOG2)
    first = jnp.min(jnp.where(needed, block, n_blocks).astype(jnp.float32))
    return jnp.minimum(first.astype(jnp.int32), limit)


def _decay_selector(u, width, th):
    row = lax.broadcasted_iota(jnp.int32, (LANES, width), 0)
    col = lax.broadcasted_iota(jnp.int32, (LANES, width), 1)
    second_map = (col & th) != 0
    first = C_PIECES * 2 * u
    owner_a = (row >= first) & (row < first + C_PIECES)
    owner_b = (row >= first + C_PIECES) & (row < first + 2 * C_PIECES)
    hit = (second_map & owner_b) | (jnp.logical_not(second_map) & owner_a)
    return jnp.where(hit, -1.0, 0.0).astype(jnp.bfloat16)


def _attn_body(is_fox, q_ref, k_ref, v_ref, ccol_ref, crow_ref, cpiece_ref,
               nsq_ref, lam_ref, g_ref, o_ref, m_scr, l_scr, acc_scr, qt_scr,
               s0_scr, s1_scr, vt_scr, bh_scr):
    tq = q_ref.shape[0]
    th = tk = ATT_TK
    n_parts = tq // th
    qi = pl.program_id(1)
    q0 = pl.multiple_of(qi * tq, tq)

    @pl.when(qi == 0)
    def _():
        seq = v_ref.shape[0]
        for u in range(UNITS):
            vt_scr[u, :PAIR, :] = v_ref[:, u * PAIR:(u + 1) * PAIR].T
            vt_scr[u, PAIR:, :] = jnp.ones((ONES_ROWS, seq), vt_scr.dtype)
            if is_fox:
                qt_scr[u, PAIR:, :] = _decay_selector(u, 2 * tq, th)
        if is_fox:
            bh_scr[...] = _logit_bound(nsq_ref)

    first_diag = n_parts * qi
    j0 = (_first_needed_block(ccol_ref, bh_scr, q0, tk, first_diag)
          if is_fox else 0)
    n_full = first_diag - j0

    lane = lax.broadcasted_iota(jnp.int32, (th, LANES), 1)
    ct = []
    for u in range(UNITS):
        pieces, ct_u = [], []
        for h in range(n_parts):
            q = q_ref[h * th:(h + 1) * th, u * PAIR:(u + 1) * PAIR]
            zero = jnp.zeros_like(q)
            pieces += [jnp.where(lane < HEAD_DIM, q, zero),
                       jnp.where(lane >= HEAD_DIM, q, zero)]
            if is_fox:
                t0 = pl.multiple_of(q0 + h * th, th)
                ct_u += [crow_ref[0, pl.ds(C_PIECES * (2 * u + m), 1),
                                  pl.ds(t0, th)] for m in range(2)]
        qt_scr[u, :PAIR, :] = jnp.concatenate(pieces, axis=0).T
        if is_fox:
            ct.append(jnp.concatenate(ct_u, axis=1))

    m_scr[...] = jnp.full_like(m_scr, MASK_VALUE)
    l_scr[...] = jnp.zeros_like(l_scr)
    acc_scr[...] = jnp.zeros_like(acc_scr)

    def logits_to(slot, j, u, lo=0):
        k0 = pl.multiple_of(j * tk, tk)
        kb = k_ref[pl.ds(k0, tk), u * PAIR:(u + 1) * PAIR]
        if is_fox:
            kb = jnp.concatenate([kb, cpiece_ref[0, pl.ds(k0, tk), :]], axis=1)
        slot[u, :, lo:] = jnp.dot(kb, qt_scr[u, :, lo:],
                                  preferred_element_type=jnp.float32)

    def softmax_pv(slot, j, u, lo, masked):
        k0 = pl.multiple_of(j * tk, tk)
        s = slot[u, :, lo:]
        width = s.shape[1]
        if masked:
            r = lax.broadcasted_iota(jnp.int32, (tk, width), 0)
            c = lax.broadcasted_iota(jnp.int32, (tk, width), 1)
            keep = r <= (c & (th - 1))
            if width > 2 * th:
                keep = keep | (c >= 2 * th)
            s = jnp.where(keep, s, MASK_VALUE)
        m_prev = m_scr[u, :, lo:]
        col_max = jnp.max(s, axis=0, keepdims=True)
        if is_fox:
            m_next = jnp.maximum(m_prev, col_max + ct[u][:, lo:])
            offset = m_next - ct[u][:, lo:]
        else:
            m_next = jnp.maximum(m_prev, col_max)
            offset = m_next
        p = jnp.exp2(s - offset).astype(vt_scr.dtype)
        alpha = jnp.exp2(m_prev - m_next)
        vt = vt_scr[u, :, pl.ds(k0, tk)]
        pv = jnp.dot(vt, p, preferred_element_type=jnp.float32)
        l_scr[u, :, lo:] = alpha * l_scr[u, :, lo:] + pv[PAIR:PAIR + 1, :]
        acc_scr[u, :, lo:] = alpha * acc_scr[u, :, lo:] + pv[:PAIR, :]
        m_scr[u, :, lo:] = m_next

    def step(j, cur, nxt, lo=0, masked=False, nxt_lo=0):
        ahead = LOGITS_LEAD
        if nxt is not None:
            for u in range(ahead):
                logits_to(nxt, j + 1, u, nxt_lo)
        for u in range(UNITS):
            softmax_pv(cur, j, u, lo, masked)
            if nxt is not None and u + ahead < UNITS:
                logits_to(nxt, j + 1, u + ahead, nxt_lo)

    for u in range(UNITS):
        logits_to(s0_scr, j0, u)

    def pair(i, carry):
        step(j0 + 2 * i, s0_scr, s1_scr)
        step(j0 + 2 * i + 1, s1_scr, s0_scr)
        return carry

    lax.fori_loop(0, lax.shift_right_logical(n_full, 1), pair, 0)

    def diagonal(cur, other):
        for k in range(n_parts):
            nxt = other if k + 1 < n_parts else None
            step(first_diag + k, cur, nxt, lo=2 * th * k, masked=True,
                 nxt_lo=2 * th * (k + 1))
            cur, other = other, cur

    if is_fox:
        @pl.when((n_full & 1) == 0)
        def _():
            diagonal(s0_scr, s1_scr)

        @pl.when((n_full & 1) == 1)
        def _():
            step(first_diag - 1, s0_scr, s1_scr)
            diagonal(s1_scr, s0_scr)
    else:
        diagonal(s0_scr, s1_scr)

    if not is_fox:
        lam_rows = lam_ref[...]
        dot = lambda i: jnp.sum(lam_rows[i:i + 1] * lam_rows[i + 1:i + 2],
                                axis=1, keepdims=True)
        lam = jnp.exp(dot(0)) - jnp.exp(dot(2)) + LAMBDA_INIT
    for u in range(UNITS):
        o_t = acc_scr[u] / l_scr[u]
        for h in range(n_parts):
            a = o_t[:, 2 * h * th:(2 * h + 1) * th]
            b = o_t[:, (2 * h + 1) * th:(2 * h + 2) * th]
            if is_fox:
                out = jnp.concatenate([a[:HEAD_DIM], b[HEAD_DIM:]], axis=0).T
            else:
                d_t = a - lam * b
                inv = lax.rsqrt(jnp.mean(d_t * d_t, axis=0, keepdims=True)
                                + RMS_EPS)
                out = (d_t * inv).T * g_ref[...] * (1.0 - LAMBDA_INIT)
            o_ref[h * th:(h + 1) * th, u * PAIR:(u + 1) * PAIR] = out.astype(
                o_ref.dtype)


def _fox_kernel(q_ref, k_ref, v_ref, ccol_ref, crow_ref, cpiece_ref, nsq_ref,
                o_ref, *scratch):
    _attn_body(True, q_ref, k_ref, v_ref, ccol_ref, crow_ref, cpiece_ref,
               nsq_ref, None, None, o_ref, *scratch)


def _diff_kernel(q_ref, k_ref, v_ref, lam_ref, g_ref, o_ref, *scratch):
    _attn_body(False, q_ref, k_ref, v_ref, None, None, None, None, lam_ref,
               g_ref, o_ref, *scratch, None)


def _attention(proj, batch, seq, group, kernel, extra_specs, extra_args,
               extra_scratch, name):
    tq = ATT_TQ[group]
    nq = seq // tq
    q_spec = pl.BlockSpec((tq, GROUP), lambda b, i: (b * nq + i, group))
    kv_spec = lambda g: pl.BlockSpec((seq, GROUP), lambda b, i: (b, g))
    out_spec = pl.BlockSpec((tq, GROUP), lambda b, i: (b * nq + i, 0))
    stat = pltpu.VMEM((UNITS, 1, 2 * tq), jnp.float32)
    acc = pltpu.VMEM((UNITS, PAIR, 2 * tq), jnp.float32)
    depth = PAIR + (LANES if group == 0 else 0)
    q_t = pltpu.VMEM((UNITS, depth, 2 * tq), jnp.bfloat16)
    logit_slot = pltpu.VMEM((UNITS, ATT_TK, 2 * tq), jnp.float32)
    v_t = pltpu.VMEM((UNITS, VT_ROWS, seq), jnp.bfloat16)
    return pl.pallas_call(
        kernel,
        grid=(batch, nq),
        in_specs=[q_spec, kv_spec(2 + group), kv_spec(4 + group)] + extra_specs,
        out_specs=out_spec,
        out_shape=jax.ShapeDtypeStruct((batch * seq, GROUP), jnp.bfloat16),
        scratch_shapes=[stat, stat, acc, q_t, logit_slot, logit_slot, v_t]
        + extra_scratch,
        compiler_params=pltpu.CompilerParams(
            dimension_semantics=("arbitrary", "arbitrary"),
            vmem_limit_bytes=VMEM_LIMIT_BYTES),
        name=name,
    )(proj, proj, proj, *extra_args)


def _rope_tables(seq):
    inv_freq = ROPE_THETA ** (-jnp.arange(0, ROPE_DIM, 2, dtype=jnp.float32)
                              / ROPE_DIM)
    p = np.arange(LANES) % HEAD_DIM
    lane_freq = jnp.where(p < ROPE_DIM, inv_freq[p % ROPE_HALF], 0.0)
    ang = jnp.arange(seq).astype(jnp.float32)[:, None] * lane_freq[None, :]
    cos, sin = jnp.cos(ang), jnp.sin(ang)
    sin_up = jnp.where((p >= ROPE_HALF) & (p < ROPE_DIM), sin, 0.0)
    sin_dn = jnp.where(p < ROPE_HALF, -sin, 0.0)
    return cos, sin_up, sin_dn


def _decay_lanes(per_head):
    rep = jnp.repeat(per_head, C_PIECES, axis=-1)
    pad = [(0, 0)] * (rep.ndim - 1) + [(0, LANES - C_LANES)]
    return jnp.pad(rep, pad)


def _gate_weight(w_in):
    gate_start = 3 * FOX_WIDTH
    fgate = w_in[:, gate_start:gate_start + N_FOX_HEADS]
    return _decay_lanes(fgate).astype(jnp.bfloat16)


def kernel(x, ffn1_pre_g, ffn1_post_g, ffn1_w_gate, ffn1_w_up, ffn1_w_down, mix_pre_g, mix_post_g, w_in, fox_forget_b, diff_lambda_q1, diff_lambda_k1, diff_lambda_q2, diff_lambda_k2, diff_subln_g, w_out, ffn2_pre_g, ffn2_post_g, ffn2_w_gate, ffn2_w_up, ffn2_w_down):
    batch, seq, d = x.shape
    bf = lambda w: w.astype(jnp.bfloat16)
    x2d = x.reshape(batch * seq, d)
    rope = _rope_tables(seq)
    for l in range(ffn1_pre_g.shape[0]):
        x2d = _ffn(x2d, ffn1_pre_g[l][None], ffn1_post_g[l][None],
                   bf(ffn1_w_gate[l]), bf(ffn1_w_up[l]), bf(ffn1_w_down[l]))

        proj, gate, nsq = _in_proj(x2d, mix_pre_g[l][None], w_in, l,
                                   _gate_weight(w_in[l]), rope, seq)
        piece_row = jnp.asarray(np.arange(LANES) % C_PIECES, jnp.int32)[None]
        ccol, crow, cpiece = _decay(gate, _decay_lanes(fox_forget_b[l])[None],
                                    piece_row, batch, seq)

        per_batch_col = pl.BlockSpec((1, seq, LANES), lambda b, i: (b, 0, 0))
        fox_o = _attention(
            proj, batch, seq, 0, _fox_kernel,
            [per_batch_col,
             pl.BlockSpec((1, C_LANES, seq), lambda b, i: (b, 0, 0)),
             per_batch_col,
             pl.BlockSpec((seq // PROJ_SUB * NSQ_ROWS, LANES),
                          lambda b, i: (b, 0))],
            [ccol, crow, cpiece, nsq],
            [pltpu.VMEM((1, LANES), jnp.float32)], "fox_attn")
        lam_rows = jnp.pad(
            jnp.stack([diff_lambda_q1[l], diff_lambda_k1[l],
                       diff_lambda_q2[l], diff_lambda_k2[l]]),
            ((0, 4), (0, LANES - HEAD_DIM)))
        diff_o = _attention(
            proj, batch, seq, 1, _diff_kernel,
            [_const_spec((8, LANES)), _const_spec((1, PAIR))],
            [lam_rows, diff_subln_g[l][None]], [], "diff_attn")

        x2d = _mix_ffn(x2d, fox_o, diff_o, bf(w_out[l]), mix_post_g[l][None],
                       ffn2_pre_g[l][None], ffn2_post_g[l][None],
                       bf(ffn2_w_gate[l]), bf(ffn2_w_up[l]), bf(ffn2_w_down[l]))
    return x2d.reshape(batch, seq, d)
```

```python
import math

import jax
import jax.numpy as jnp
import numpy as np
from jax import lax
from jax.experimental import pallas as pl
from jax.experimental.pallas import tpu as pltpu

D_MODEL = 1024
N_FOX_HEADS = 8
N_DIFF_HEADS = 4
HEAD_DIM = 64
PAIR = 2 * HEAD_DIM
FOX_WIDTH = N_FOX_HEADS * HEAD_DIM
DIFF_WIDTH = N_DIFF_HEADS * PAIR
ROPE_THETA = 500000.0
ROPE_DIM = HEAD_DIM // 4
ROPE_HALF = ROPE_DIM // 2
D_FF = 2816
FFN_RESIDUAL_WEIGHT = 0.5
RMS_EPS = 1e-6
LAMBDA_INIT = 0.8 - 0.6 * math.exp(-0.3 * 0)
QK_SCALE = HEAD_DIM ** -0.5
LOG2E = math.log2(math.e)

LANES = 128
MASK_VALUE = -1e30
VMEM_LIMIT_BYTES = 56 * 1024 * 1024

GROUP = FOX_WIDTH
PROJ_WIDTH = 6 * GROUP
UNITS = GROUP // PAIR


def _rms(x, g):
    return x * lax.rsqrt(jnp.mean(x * x, axis=-1, keepdims=True) + RMS_EPS) * g


def _const_spec(shape):
    return pl.BlockSpec(shape, lambda *_: (0,) * len(shape),
                        pipeline_mode=pl.Buffered(1))


FFN_TM = 1024
FFN_SUB = 256
FFN_CHUNKS = ((0, 1536), (1536, 1280))


def _swiglu_half_step(x, pre_g, post_g, wg_ref, wu_ref, wd_ref):
    xn = _rms(x, pre_g).astype(jnp.bfloat16)
    acc = None
    for start, width in FFN_CHUNKS:
        g = jnp.dot(xn, wg_ref[:, start:start + width],
                    preferred_element_type=jnp.float32)
        u = jnp.dot(xn, wu_ref[:, start:start + width],
                    preferred_element_type=jnp.float32)
        h = (g * jax.nn.sigmoid(g) * u).astype(jnp.bfloat16)
        part = jnp.dot(h, wd_ref[start:start + width, :],
                       preferred_element_type=jnp.float32)
        acc = part if acc is None else acc + part
    return x + FFN_RESIDUAL_WEIGHT * _rms(acc, post_g)


def _ffn_kernel(x_ref, pre_g_ref, post_g_ref, wg_ref, wu_ref, wd_ref, o_ref):
    for r in range(0, FFN_TM, FFN_SUB):
        o_ref[r:r + FFN_SUB, :] = _swiglu_half_step(
            x_ref[r:r + FFN_SUB, :], pre_g_ref[...], post_g_ref[...],
            wg_ref, wu_ref, wd_ref)


def _mix_ffn_kernel(x_ref, fox_ref, diff_ref, wout_ref, mix_g_ref,
                    pre_g_ref, post_g_ref, wg_ref, wu_ref, wd_ref, o_ref):
    def mixed(r):
        rows = slice(r, r + FFN_SUB)
        m = jnp.dot(fox_ref[rows, :], wout_ref[:FOX_WIDTH, :],
                    preferred_element_type=jnp.float32)
        m = m + jnp.dot(diff_ref[rows, :], wout_ref[FOX_WIDTH:, :],
                        preferred_element_type=jnp.float32)
        return x_ref[rows, :] + _rms(m, mix_g_ref[...])

    x1_next = mixed(0)
    for r in range(0, FFN_TM, FFN_SUB):
        x1 = x1_next
        if r + FFN_SUB < FFN_TM:
            x1_next = mixed(r + FFN_SUB)
        o_ref[r:r + FFN_SUB, :] = _swiglu_half_step(
            x1, pre_g_ref[...], post_g_ref[...], wg_ref, wu_ref, wd_ref)


def _ffn_weight_specs():
    return [_const_spec((1, D_MODEL)), _const_spec((1, D_MODEL)),
            _const_spec((D_MODEL, D_FF)), _const_spec((D_MODEL, D_FF)),
            _const_spec((D_FF, D_MODEL))]


def _ffn(x2d, pre_g, post_g, wg, wu, wd):
    t = x2d.shape[0]
    row = pl.BlockSpec((FFN_TM, D_MODEL), lambda i: (i, 0))
    return pl.pallas_call(
        _ffn_kernel,
        grid=(t // FFN_TM,),
        in_specs=[row] + _ffn_weight_specs(),
        out_specs=row,
        out_shape=jax.ShapeDtypeStruct(x2d.shape, x2d.dtype),
        compiler_params=pltpu.CompilerParams(
            dimension_semantics=("arbitrary",),
            vmem_limit_bytes=VMEM_LIMIT_BYTES),
        name="ffn",
    )(x2d, pre_g, post_g, wg, wu, wd)


def _mix_ffn(x2d, fox_o, diff_o, w_out, mix_g, pre_g, post_g, wg, wu, wd):
    t = x2d.shape[0]
    row = lambda w: pl.BlockSpec((FFN_TM, w), lambda i: (i, 0))
    return pl.pallas_call(
        _mix_ffn_kernel,
        grid=(t // FFN_TM,),
        in_specs=[row(D_MODEL), row(FOX_WIDTH), row(DIFF_WIDTH),
                  _const_spec((FOX_WIDTH + DIFF_WIDTH, D_MODEL)),
                  _const_spec((1, D_MODEL))] + _ffn_weight_specs(),
        out_specs=row(D_MODEL),
        out_shape=jax.ShapeDtypeStruct(x2d.shape, x2d.dtype),
        compiler_params=pltpu.CompilerParams(
            dimension_semantics=("arbitrary",),
            vmem_limit_bytes=VMEM_LIMIT_BYTES),
        name="mix_ffn",
    )(x2d, fox_o, diff_o, w_out, mix_g, pre_g, post_g, wg, wu, wd)


PROJ_TM = 1024
PROJ_SUB = 256
ROPE_GROUPS = (1, 3)
Q_GROUPS = (0, 1)
FOX_QK_GROUPS = (0, 2)
_DIFF_START = 3 * FOX_WIDTH + N_FOX_HEADS
W_IN_GROUP_STARTS = (0, _DIFF_START, FOX_WIDTH, _DIFF_START + DIFF_WIDTH,
                     2 * FOX_WIDTH, _DIFF_START + 2 * DIFF_WIDTH)
W_CAST_ROWS = 128
NSQ_ROWS = 2 * UNITS


def _rope_slab(y, cos, sin_up, sin_dn):
    up = pltpu.roll(y, ROPE_HALF, 1)
    dn = pltpu.roll(y, LANES - ROPE_HALF, 1)
    return y * cos + up * sin_up + dn * sin_dn


def _max_head_sqnorms(slab):
    lane = lax.broadcasted_iota(jnp.int32, slab.shape, 1)
    sq = slab * slab
    head = lambda keep: jnp.max(
        jnp.sum(jnp.where(keep, sq, 0.0), axis=1, keepdims=True),
        axis=0, keepdims=True)
    lane1 = lax.broadcasted_iota(jnp.int32, (1, LANES), 1)
    return jnp.where(lane1 == 0, head(lane < HEAD_DIM),
                     jnp.where(lane1 == 1, head(lane >= HEAD_DIM), 0.0))


def _in_proj_kernel(x_ref, g_ref, win_ref, wgate_ref, cos_ref, sup_ref, sdn_ref,
                    proj_ref, gate_ref, nsq_ref, w_scr):
    @pl.when(pl.program_id(0) == 0)
    def _():
        for dst, src in enumerate(W_IN_GROUP_STARTS):
            for r in range(0, D_MODEL, W_CAST_ROWS):
                w_scr[r:r + W_CAST_ROWS, dst * GROUP:(dst + 1) * GROUP] = (
                    win_ref[0, r:r + W_CAST_ROWS, src:src + GROUP].astype(
                        w_scr.dtype))

    for r in range(0, PROJ_TM, PROJ_SUB):
        rows = slice(r, r + PROJ_SUB)
        xn = _rms(x_ref[rows, :], g_ref[...]).astype(jnp.bfloat16)
        cos, sup, sdn = cos_ref[rows, :], sup_ref[rows, :], sdn_ref[rows, :]
        sqnorms = []
        for c in range(PROJ_WIDTH // GROUP):
            lo = c * GROUP
            y = jnp.dot(xn, w_scr[:, lo:lo + GROUP],
                        preferred_element_type=jnp.float32)
            for s in range(UNITS):
                slab = y[:, s * LANES:(s + 1) * LANES]
                if c in ROPE_GROUPS:
                    slab = _rope_slab(slab, cos, sup, sdn)
                if c in Q_GROUPS:
                    slab = slab * (QK_SCALE * LOG2E)
                slab = slab.astype(proj_ref.dtype)
                proj_ref[rows, lo + s * LANES:lo + (s + 1) * LANES] = slab
                if c in FOX_QK_GROUPS:
                    sqnorms.append(_max_head_sqnorms(slab.astype(jnp.float32)))
        gate_ref[rows, :] = jnp.dot(xn, wgate_ref[...],
                                    preferred_element_type=jnp.float32)
        n0 = (r // PROJ_SUB) * NSQ_ROWS
        nsq_ref[n0:n0 + NSQ_ROWS, :] = jnp.concatenate(sqnorms, axis=0)


def _in_proj(x2d, pre_g, w_in, layer, w_gate, rope, seq):
    t = x2d.shape[0]
    tiles_per_seq = seq // PROJ_TM
    row = lambda w: pl.BlockSpec((PROJ_TM, w), lambda i: (i, 0))
    table = pl.BlockSpec((PROJ_TM, LANES), lambda i: (i % tiles_per_seq, 0))
    w_spec = pl.BlockSpec((1,) + w_in.shape[1:], lambda i: (layer, 0, 0),
                          pipeline_mode=pl.Buffered(1))
    nsq_rows = PROJ_TM // PROJ_SUB * NSQ_ROWS
    return pl.pallas_call(
        _in_proj_kernel,
        grid=(t // PROJ_TM,),
        in_specs=[row(D_MODEL), _const_spec((1, D_MODEL)), w_spec,
                  _const_spec((D_MODEL, LANES)), table, table, table],
        scratch_shapes=[pltpu.VMEM((D_MODEL, PROJ_WIDTH), jnp.bfloat16)],
        out_specs=[row(PROJ_WIDTH), row(LANES),
                   pl.BlockSpec((nsq_rows, LANES), lambda i: (i, 0))],
        out_shape=[jax.ShapeDtypeStruct((t, PROJ_WIDTH), jnp.bfloat16),
                   jax.ShapeDtypeStruct((t, LANES), jnp.float32),
                   jax.ShapeDtypeStruct((t // PROJ_TM * nsq_rows, LANES),
                                        jnp.float32)],
        compiler_params=pltpu.CompilerParams(
            dimension_semantics=("arbitrary",),
            vmem_limit_bytes=VMEM_LIMIT_BYTES),
        name="in_proj",
    )(x2d, pre_g, w_in, w_gate, *rope)


C_PIECES = 3
C_LANES = N_FOX_HEADS * C_PIECES


def _decay_kernel(gate_ref, b_ref, piece_ref, ccol_ref, crow_ref, cpiece_ref):
    z = gate_ref[0] + b_ref[...]
    c = jnp.minimum(z, 0.0) - jnp.log1p(jnp.exp(-jnp.abs(z)))
    seq = c.shape[0]
    rows = lax.broadcasted_iota(jnp.int32, c.shape, 0)
    shift = 1
    while shift < seq:
        c = c + jnp.where(rows >= shift, pltpu.roll(c, shift, 0), 0.0)
        shift *= 2
    c = c * LOG2E
    ccol_ref[0] = c
    crow_ref[0] = c.T[:C_LANES, :]
    hi = c.astype(jnp.bfloat16)
    rest = c - hi.astype(jnp.float32)
    mid = rest.astype(jnp.bfloat16)
    lo = (rest - mid.astype(jnp.float32)).astype(jnp.bfloat16)
    piece = piece_ref[...]
    cpiece_ref[0] = jnp.where(piece == 0, hi, jnp.where(piece == 1, mid, lo))


def _decay(gate, bias_row, piece_row, batch, seq):
    gate3 = gate.reshape(batch, seq, LANES)
    col = pl.BlockSpec((1, seq, LANES), lambda b: (b, 0, 0))
    row = pl.BlockSpec((1, C_LANES, seq), lambda b: (b, 0, 0))
    return pl.pallas_call(
        _decay_kernel,
        grid=(batch,),
        in_specs=[col, _const_spec((1, LANES)), _const_spec((1, LANES))],
        out_specs=[col, row, col],
        out_shape=[jax.ShapeDtypeStruct((batch, seq, LANES), jnp.float32),
                   jax.ShapeDtypeStruct((batch, C_LANES, seq), jnp.float32),
                   jax.ShapeDtypeStruct((batch, seq, LANES), jnp.bfloat16)],
        compiler_params=pltpu.CompilerParams(
            dimension_semantics=("arbitrary",),
            vmem_limit_bytes=VMEM_LIMIT_BYTES),
        name="decay",
    )(gate3, bias_row, piece_row)


ATT_TK = 256
ATT_TQ = (2 * ATT_TK, 4 * ATT_TK)
LOGITS_LEAD = 1
ONES_ROWS = 16
VT_ROWS = PAIR + ONES_ROWS
SKIP_LOG2 = 160.0
BOUND_SLACK = 1.02


def _logit_bound(nsq_ref):
    n = nsq_ref[0:NSQ_ROWS, :]
    for r in range(NSQ_ROWS, nsq_ref.shape[0], NSQ_ROWS):
        n = jnp.maximum(n, nsq_ref[r:r + NSQ_ROWS, :])
    bound = jnp.sqrt(n[:UNITS] * n[UNITS:]) * BOUND_SLACK
    lane = lax.broadcasted_iota(jnp.int32, (1, LANES), 1)
    out = jnp.zeros((1, LANES), jnp.float32)
    for h in range(N_FOX_HEADS):
        value = jnp.sum(jnp.where(lane == h % 2, bound[h // 2:h // 2 + 1], 0.0),
                        axis=1, keepdims=True)
        mine = (lane >= C_PIECES * h) & (lane < C_PIECES * (h + 1))
        out = jnp.where(mine, value, out)
    return out


def _first_needed_block(ccol_ref, bh_scr, q0, tk, limit):
    n_blocks = ccol_ref.shape[1] // tk
    c_end = ccol_ref[0, pl.ds(tk - 1, n_blocks, stride=tk), :]
    c_q = ccol_ref[0, pl.ds(q0, 1), :]
    bound = 2.0 * bh_scr[...] + (c_q - c_end)
    block = lax.broadcasted_iota(jnp.int32, bound.shape, 0)
    lane = lax.broadcasted_iota(jnp.int32, bound.shape, 1)
    needed = (lane < C_LANES) & jnp.logical_not(bound < -SKIP_LOG2)
    first = jnp.min(jnp.where(needed, block, n_blocks).astype(jnp.float32))
    return jnp.minimum(first.astype(jnp.int32), limit)


def _decay_selector(u, width, th):
    row = lax.broadcasted_iota(jnp.int32, (LANES, width), 0)
    col = lax.broadcasted_iota(jnp.int32, (LANES, width), 1)
    second_map = (col & th) != 0
    first = C_PIECES * 2 * u
    owner_a = (row >= first) & (row < first + C_PIECES)
    owner_b = (row >= first + C_PIECES) & (row < first + 2 * C_PIECES)
    hit = (second_map & owner_b) | (jnp.logical_not(second_map) & owner_a)
    return jnp.where(hit, -1.0, 0.0).astype(jnp.bfloat16)


def _attn_body(is_fox, q_ref, k_ref, v_ref, ccol_ref, crow_ref, cpiece_ref,
               nsq_ref, lam_ref, g_ref, o_ref, m_scr, l_scr, acc_scr, qt_scr,
               s0_scr, s1_scr, vt_scr, bh_scr):
    tq = q_ref.shape[0]
    th = tk = ATT_TK
    n_parts = tq // th
    qi = pl.program_id(1)
    q0 = pl.multiple_of(qi * tq, tq)

    @pl.when(qi == 0)
    def _():
        seq = v_ref.shape[0]
        for u in range(UNITS):
            vt_scr[u, :PAIR, :] = v_ref[:, u * PAIR:(u + 1) * PAIR].T
            vt_scr[u, PAIR:, :] = jnp.ones((ONES_ROWS, seq), vt_scr.dtype)
            if is_fox:
                qt_scr[u, PAIR:, :] = _decay_selector(u, 2 * tq, th)
        if is_fox:
            bh_scr[...] = _logit_bound(nsq_ref)

    first_diag = n_parts * qi
    j0 = (_first_needed_block(ccol_ref, bh_scr, q0, tk, first_diag)
          if is_fox else 0)
    n_full = first_diag - j0

    lane = lax.broadcasted_iota(jnp.int32, (th, LANES), 1)
    ct = []
    for u in range(UNITS):
        pieces, ct_u = [], []
        for h in range(n_parts):
            q = q_ref[h * th:(h + 1) * th, u * PAIR:(u + 1) * PAIR]
            zero = jnp.zeros_like(q)
            pieces += [jnp.where(lane < HEAD_DIM, q, zero),
                       jnp.where(lane >= HEAD_DIM, q, zero)]
            if is_fox:
                t0 = pl.multiple_of(q0 + h * th, th)
                ct_u += [crow_ref[0, pl.ds(C_PIECES * (2 * u + m), 1),
                                  pl.ds(t0, th)] for m in range(2)]
        qt_scr[u, :PAIR, :] = jnp.concatenate(pieces, axis=0).T
        if is_fox:
            ct.append(jnp.concatenate(ct_u, axis=1))

    m_scr[...] = jnp.full_like(m_scr, MASK_VALUE)
    l_scr[...] = jnp.zeros_like(l_scr)
    acc_scr[...] = jnp.zeros_like(acc_scr)

    def logits_to(slot, j, u, lo=0, hi=None):
        k0 = pl.multiple_of(j * tk, tk)
        kb = k_ref[pl.ds(k0, tk), u * PAIR:(u + 1) * PAIR]
        if is_fox:
            kb = jnp.concatenate([kb, cpiece_ref[0, pl.ds(k0, tk), :]], axis=1)
        slot[u, :, lo:hi] = jnp.dot(kb, qt_scr[u, :, lo:hi],
                                    preferred_element_type=jnp.float32)

    def softmax_pv(slot, j, u, lo, hi, masked):
        k0 = pl.multiple_of(j * tk, tk)
        s = slot[u, :, lo:hi]
        width = s.shape[1]
        if masked:
            r = lax.broadcasted_iota(jnp.int32, (tk, width), 0)
            c = lax.broadcasted_iota(jnp.int32, (tk, width), 1)
            keep = r <= (c & (th - 1))
            if width > 2 * th:
                keep = keep | (c >= 2 * th)
            s = jnp.where(keep, s, MASK_VALUE)
        m_prev = m_scr[u, :, lo:hi]
        col_max = jnp.max(s, axis=0, keepdims=True)
        if is_fox:
            m_next = jnp.maximum(m_prev, col_max + ct[u][:, lo:hi])
            offset = m_next - ct[u][:, lo:hi]
        else:
            m_next = jnp.maximum(m_prev, col_max)
            offset = m_next
        p = jnp.exp2(s - offset).astype(vt_scr.dtype)
        alpha = jnp.exp2(m_prev - m_next)
        vt = vt_scr[u, :, pl.ds(k0, tk)]
        pv = jnp.dot(vt, p, preferred_element_type=jnp.float32)
        l_scr[u, :, lo:hi] = alpha * l_scr[u, :, lo:hi] + pv[PAIR:PAIR + 1, :]
        acc_scr[u, :, lo:hi] = alpha * acc_scr[u, :, lo:hi] + pv[:PAIR, :]
        m_scr[u, :, lo:hi] = m_next

    def step(j, cur, nxt, lo=0, hi=None, masked=False, nxt_lo=0):
        ahead = LOGITS_LEAD
        if nxt is not None:
            for u in range(ahead):
                logits_to(nxt, j + 1, u, nxt_lo)
        for u in range(UNITS):
            softmax_pv(cur, j, u, lo, hi, masked)
            if nxt is not None and u + ahead < UNITS:
                logits_to(nxt, j + 1, u + ahead, nxt_lo)

    if is_fox:
        j1 = _first_needed_block(ccol_ref, bh_scr, q0 + th, tk, first_diag)
        narrow = (n_full == 1) & (j1 == first_diag)
        wide = jnp.logical_not(narrow)

        @pl.when(narrow)
        def _():
            for u in range(UNITS):
                logits_to(s0_scr, j0, u, hi=2 * th)

        @pl.when(wide)
        def _():
            for u in range(UNITS):
                logits_to(s0_scr, j0, u)
    else:
        for u in range(UNITS):
            logits_to(s0_scr, j0, u)

    def pair(i, carry):
        step(j0 + 2 * i, s0_scr, s1_scr)
        step(j0 + 2 * i + 1, s1_scr, s0_scr)
        return carry

    lax.fori_loop(0, lax.shift_right_logical(n_full, 1), pair, 0)

    def diagonal(cur, other):
        for k in range(n_parts):
            nxt = other if k + 1 < n_parts else None
            step(first_diag + k, cur, nxt, lo=2 * th * k, masked=True,
                 nxt_lo=2 * th * (k + 1))
            cur, other = other, cur

    if is_fox:
        @pl.when(narrow)
        def _():
            step(first_diag - 1, s0_scr, s1_scr, hi=2 * th)
            diagonal(s1_scr, s0_scr)

        @pl.when(wide & ((n_full & 1) == 0))
        def _():
            diagonal(s0_scr, s1_scr)

        @pl.when(wide & ((n_full & 1) == 1))
        def _():
            step(first_diag - 1, s0_scr, s1_scr)
            diagonal(s1_scr, s0_scr)
    else:
        diagonal(s0_scr, s1_scr)

    if not is_fox:
        lam_rows = lam_ref[...]
        dot = lambda i: jnp.sum(lam_rows[i:i + 1] * lam_rows[i + 1:i + 2],
                                axis=1, keepdims=True)
        lam = jnp.exp(dot(0)) - jnp.exp(dot(2)) + LAMBDA_INIT
    for u in range(UNITS):
        o_t = acc_scr[u] / l_scr[u]
        for h in range(n_parts):
            a = o_t[:, 2 * h * th:(2 * h + 1) * th]
            b = o_t[:, (2 * h + 1) * th:(2 * h + 2) * th]
            if is_fox:
                out = jnp.concatenate([a[:HEAD_DIM], b[HEAD_DIM:]], axis=0).T
            else:
                d_t = a - lam * b
                inv = lax.rsqrt(jnp.mean(d_t * d_t, axis=0, keepdims=True)
                                + RMS_EPS)
                out = (d_t * inv).T * g_ref[...] * (1.0 - LAMBDA_INIT)
            o_ref[h * th:(h + 1) * th, u * PAIR:(u + 1) * PAIR] = out.astype(
                o_ref.dtype)


def _fox_kernel(q_ref, k_ref, v_ref, ccol_ref, crow_ref, cpiece_ref, nsq_ref,
                o_ref, *scratch):
    _attn_body(True, q_ref, k_ref, v_ref, ccol_ref, crow_ref, cpiece_ref,
               nsq_ref, None, None, o_ref, *scratch)


def _diff_kernel(q_ref, k_ref, v_ref, lam_ref, g_ref, o_ref, *scratch):
    _attn_body(False, q_ref, k_ref, v_ref, None, None, None, None, lam_ref,
               g_ref, o_ref, *scratch, None)


def _attention(proj, batch, seq, group, kernel, extra_specs, extra_args,
               extra_scratch, name):
    tq = ATT_TQ[group]
    nq = seq // tq
    q_spec = pl.BlockSpec((tq, GROUP), lambda b, i: (b * nq + i, group))
    kv_spec = lambda g: pl.BlockSpec((seq, GROUP), lambda b, i: (b, g))
    out_spec = pl.BlockSpec((tq, GROUP), lambda b, i: (b * nq + i, 0))
    stat = pltpu.VMEM((UNITS, 1, 2 * tq), jnp.float32)
    acc = pltpu.VMEM((UNITS, PAIR, 2 * tq), jnp.float32)
    depth = PAIR + (LANES if group == 0 else 0)
    q_t = pltpu.VMEM((UNITS, depth, 2 * tq), jnp.bfloat16)
    logit_slot = pltpu.VMEM((UNITS, ATT_TK, 2 * tq), jnp.float32)
    v_t = pltpu.VMEM((UNITS, VT_ROWS, seq), jnp.bfloat16)
    return pl.pallas_call(
        kernel,
        grid=(batch, nq),
        in_specs=[q_spec, kv_spec(2 + group), kv_spec(4 + group)] + extra_specs,
        out_specs=out_spec,
        out_shape=jax.ShapeDtypeStruct((batch * seq, GROUP), jnp.bfloat16),
        scratch_shapes=[stat, stat, acc, q_t, logit_slot, logit_slot, v_t]
        + extra_scratch,
        compiler_params=pltpu.CompilerParams(
            dimension_semantics=("arbitrary", "arbitrary"),
            vmem_limit_bytes=VMEM_LIMIT_BYTES),
        name=name,
    )(proj, proj, proj, *extra_args)


def _rope_tables(seq):
    inv_freq = ROPE_THETA ** (-jnp.arange(0, ROPE_DIM, 2, dtype=jnp.float32)
                              / ROPE_DIM)
    p = np.arange(LANES) % HEAD_DIM
    lane_freq = jnp.where(p < ROPE_DIM, inv_freq[p % ROPE_HALF], 0.0)
    ang = jnp.arange(seq).astype(jnp.float32)[:, None] * lane_freq[None, :]
    cos, sin = jnp.cos(ang), jnp.sin(ang)
    sin_up = jnp.where((p >= ROPE_HALF) & (p < ROPE_DIM), sin, 0.0)
    sin_dn = jnp.where(p < ROPE_HALF, -sin, 0.0)
    return cos, sin_up, sin_dn


def _decay_lanes(per_head):
    rep = jnp.repeat(per_head, C_PIECES, axis=-1)
    pad = [(0, 0)] * (rep.ndim - 1) + [(0, LANES - C_LANES)]
    return jnp.pad(rep, pad)


def _gate_weight(w_in):
    gate_start = 3 * FOX_WIDTH
    fgate = w_in[:, gate_start:gate_start + N_FOX_HEADS]
    return _decay_lanes(fgate).astype(jnp.bfloat16)


def kernel(x, ffn1_pre_g, ffn1_post_g, ffn1_w_gate, ffn1_w_up, ffn1_w_down, mix_pre_g, mix_post_g, w_in, fox_forget_b, diff_lambda_q1, diff_lambda_k1, diff_lambda_q2, diff_lambda_k2, diff_subln_g, w_out, ffn2_pre_g, ffn2_post_g, ffn2_w_gate, ffn2_w_up, ffn2_w_down):
    batch, seq, d = x.shape
    bf = lambda w: w.astype(jnp.bfloat16)
    x2d = x.reshape(batch * seq, d)
    rope = _rope_tables(seq)
    for l in range(ffn1_pre_g.shape[0]):
        x2d = _ffn(x2d, ffn1_pre_g[l][None], ffn1_post_g[l][None],
                   bf(ffn1_w_gate[l]), bf(ffn1_w_up[l]), bf(ffn1_w_down[l]))

        proj, gate, nsq = _in_proj(x2d, mix_pre_g[l][None], w_in, l,
                                   _gate_weight(w_in[l]), rope, seq)
        piece_row = jnp.asarray(np.arange(LANES) % C_PIECES, jnp.int32)[None]
        ccol, crow, cpiece = _decay(gate, _decay_lanes(fox_forget_b[l])[None],
                                    piece_row, batch, seq)

        per_batch_col = pl.BlockSpec((1, seq, LANES), lambda b, i: (b, 0, 0))
        fox_o = _attention(
            proj, batch, seq, 0, _fox_kernel,
            [per_batch_col,
             pl.BlockSpec((1, C_LANES, seq), lambda b, i: (b, 0, 0)),
             per_batch_col,
             pl.BlockSpec((seq // PROJ_SUB * NSQ_ROWS, LANES),
                          lambda b, i: (b, 0))],
            [ccol, crow, cpiece, nsq],
            [pltpu.VMEM((1, LANES), jnp.float32)], "fox_attn")
        lam_rows = jnp.pad(
            jnp.stack([diff_lambda_q1[l], diff_lambda_k1[l],
                       diff_lambda_q2[l], diff_lambda_k2[l]]),
            ((0, 4), (0, LANES - HEAD_DIM)))
        diff_o = _attention(
            proj, batch, seq, 1, _diff_kernel,
            [_const_spec((8, LANES)), _const_spec((1, PAIR))],
            [lam_rows, diff_subln_g[l][None]], [], "diff_attn")

        x2d = _mix_ffn(x2d, fox_o, diff_o, bf(w_out[l]), mix_post_g[l][None],
                       ffn2_pre_g[l][None], ffn2_post_g[l][None],
                       bf(ffn2_w_gate[l]), bf(ffn2_w_up[l]), bf(ffn2_w_down[l]))
    return x2d.reshape(batch, seq, d)
```

```python
import math

import jax
import jax.numpy as jnp
import numpy as np
from jax import lax
from jax.experimental import pallas as pl
from jax.experimental.pallas import tpu as pltpu

D_MODEL = 1024
N_FOX_HEADS = 8
N_DIFF_HEADS = 4
HEAD_DIM = 64
PAIR = 2 * HEAD_DIM
FOX_WIDTH = N_FOX_HEADS * HEAD_DIM
DIFF_WIDTH = N_DIFF_HEADS * PAIR
ROPE_THETA = 500000.0
ROPE_DIM = HEAD_DIM // 4
ROPE_HALF = ROPE_DIM // 2
D_FF = 2816
FFN_RESIDUAL_WEIGHT = 0.5
RMS_EPS = 1e-6
LAMBDA_INIT = 0.8 - 0.6 * math.exp(-0.3 * 0)
QK_SCALE = HEAD_DIM ** -0.5
LOG2E = math.log2(math.e)

LANES = 128
MASK_VALUE = -1e30
VMEM_LIMIT_BYTES = 56 * 1024 * 1024

GROUP = FOX_WIDTH
PROJ_WIDTH = 6 * GROUP
UNITS = GROUP // PAIR


def _rms(x, g):
    return x * lax.rsqrt(jnp.mean(x * x, axis=-1, keepdims=True) + RMS_EPS) * g


def _const_spec(shape):
    return pl.BlockSpec(shape, lambda *_: (0,) * len(shape),
                        pipeline_mode=pl.Buffered(1))


FFN_TM = 1024
FFN_SUB = 256
FFN_CHUNKS = ((0, 1536), (1536, 1280))


def _swiglu_half_step(x, pre_g, post_g, wg_ref, wu_ref, wd_ref):
    xn = _rms(x, pre_g).astype(jnp.bfloat16)
    acc = None
    for start, width in FFN_CHUNKS:
        g = jnp.dot(xn, wg_ref[:, start:start + width],
                    preferred_element_type=jnp.float32)
        u = jnp.dot(xn, wu_ref[:, start:start + width],
                    preferred_element_type=jnp.float32)
        h = (g * jax.nn.sigmoid(g) * u).astype(jnp.bfloat16)
        part = jnp.dot(h, wd_ref[start:start + width, :],
                       preferred_element_type=jnp.float32)
        acc = part if acc is None else acc + part
    return x + FFN_RESIDUAL_WEIGHT * _rms(acc, post_g)


def _ffn_kernel(x_ref, pre_g_ref, post_g_ref, wg_ref, wu_ref, wd_ref, o_ref):
    for r in range(0, FFN_TM, FFN_SUB):
        o_ref[r:r + FFN_SUB, :] = _swiglu_half_step(
            x_ref[r:r + FFN_SUB, :], pre_g_ref[...], post_g_ref[...],
            wg_ref, wu_ref, wd_ref)


def _mix_ffn_kernel(x_ref, fox_ref, diff_ref, wout_ref, mix_g_ref,
                    pre_g_ref, post_g_ref, wg_ref, wu_ref, wd_ref, o_ref):
    def mixed(r):
        rows = slice(r, r + FFN_SUB)
        m = jnp.dot(fox_ref[rows, :], wout_ref[:FOX_WIDTH, :],
                    preferred_element_type=jnp.float32)
        m = m + jnp.dot(diff_ref[rows, :], wout_ref[FOX_WIDTH:, :],
                        preferred_element_type=jnp.float32)
        return x_ref[rows, :] + _rms(m, mix_g_ref[...])

    x1_next = mixed(0)
    for r in range(0, FFN_TM, FFN_SUB):
        x1 = x1_next
        if r + FFN_SUB < FFN_TM:
            x1_next = mixed(r + FFN_SUB)
        o_ref[r:r + FFN_SUB, :] = _swiglu_half_step(
            x1, pre_g_ref[...], post_g_ref[...], wg_ref, wu_ref, wd_ref)


def _ffn_weight_specs():
    return [_const_spec((1, D_MODEL)), _const_spec((1, D_MODEL)),
            _const_spec((D_MODEL, D_FF)), _const_spec((D_MODEL, D_FF)),
            _const_spec((D_FF, D_MODEL))]


def _ffn(x2d, pre_g, post_g, wg, wu, wd):
    t = x2d.shape[0]
    row = pl.BlockSpec((FFN_TM, D_MODEL), lambda i: (i, 0))
    return pl.pallas_call(
        _ffn_kernel,
        grid=(t // FFN_TM,),
        in_specs=[row] + _ffn_weight_specs(),
        out_specs=row,
        out_shape=jax.ShapeDtypeStruct(x2d.shape, x2d.dtype),
        compiler_params=pltpu.CompilerParams(
            dimension_semantics=("arbitrary",),
            vmem_limit_bytes=VMEM_LIMIT_BYTES),
        name="ffn",
    )(x2d, pre_g, post_g, wg, wu, wd)


def _mix_ffn(x2d, fox_o, diff_o, w_out, mix_g, pre_g, post_g, wg, wu, wd):
    t = x2d.shape[0]
    row = lambda w: pl.BlockSpec((FFN_TM, w), lambda i: (i, 0))
    return pl.pallas_call(
        _mix_ffn_kernel,
        grid=(t // FFN_TM,),
        in_specs=[row(D_MODEL), row(FOX_WIDTH), row(DIFF_WIDTH),
                  _const_spec((FOX_WIDTH + DIFF_WIDTH, D_MODEL)),
                  _const_spec((1, D_MODEL))] + _ffn_weight_specs(),
        out_specs=row(D_MODEL),
        out_shape=jax.ShapeDtypeStruct(x2d.shape, x2d.dtype),
        compiler_params=pltpu.CompilerParams(
            dimension_semantics=("arbitrary",),
            vmem_limit_bytes=VMEM_LIMIT_BYTES),
        name="mix_ffn",
    )(x2d, fox_o, diff_o, w_out, mix_g, pre_g, post_g, wg, wu, wd)


PROJ_TM = 1024
PROJ_SUB = 256
ROPE_GROUPS = (1, 3)
Q_GROUPS = (0, 1)
FOX_QK_GROUPS = (0, 2)
_DIFF_START = 3 * FOX_WIDTH + N_FOX_HEADS
W_IN_GROUP_STARTS = (0, _DIFF_START, FOX_WIDTH, _DIFF_START + DIFF_WIDTH,
                     2 * FOX_WIDTH, _DIFF_START + 2 * DIFF_WIDTH)
W_CAST_ROWS = 128
NSQ_ROWS = 2 * UNITS


def _rope_slab(y, cos, sin_up, sin_dn):
    up = pltpu.roll(y, ROPE_HALF, 1)
    dn = pltpu.roll(y, LANES - ROPE_HALF, 1)
    return y * cos + up * sin_up + dn * sin_dn


def _max_head_sqnorms(slab):
    lane = lax.broadcasted_iota(jnp.int32, slab.shape, 1)
    sq = slab * slab
    head = lambda keep: jnp.max(
        jnp.sum(jnp.where(keep, sq, 0.0), axis=1, keepdims=True),
        axis=0, keepdims=True)
    lane1 = lax.broadcasted_iota(jnp.int32, (1, LANES), 1)
    return jnp.where(lane1 == 0, head(lane < HEAD_DIM),
                     jnp.where(lane1 == 1, head(lane >= HEAD_DIM), 0.0))


def _in_proj_kernel(x_ref, g_ref, win_ref, wgate_ref, cos_ref, sup_ref, sdn_ref,
                    proj_ref, gate_ref, nsq_ref, w_scr):
    @pl.when(pl.program_id(0) == 0)
    def _():
        for dst, src in enumerate(W_IN_GROUP_STARTS):
            for r in range(0, D_MODEL, W_CAST_ROWS):
                w_scr[r:r + W_CAST_ROWS, dst * GROUP:(dst + 1) * GROUP] = (
                    win_ref[0, r:r + W_CAST_ROWS, src:src + GROUP].astype(
                        w_scr.dtype))

    for r in range(0, PROJ_TM, PROJ_SUB):
        rows = slice(r, r + PROJ_SUB)
        xn = _rms(x_ref[rows, :], g_ref[...]).astype(jnp.bfloat16)
        cos, sup, sdn = cos_ref[rows, :], sup_ref[rows, :], sdn_ref[rows, :]
        sqnorms = []
        for c in range(PROJ_WIDTH // GROUP):
            lo = c * GROUP
            y = jnp.dot(xn, w_scr[:, lo:lo + GROUP],
                        preferred_element_type=jnp.float32)
            for s in range(UNITS):
                slab = y[:, s * LANES:(s + 1) * LANES]
                if c in ROPE_GROUPS:
                    slab = _rope_slab(slab, cos, sup, sdn)
                if c in Q_GROUPS:
                    slab = slab * (QK_SCALE * LOG2E)
                slab = slab.astype(proj_ref.dtype)
                proj_ref[rows, lo + s * LANES:lo + (s + 1) * LANES] = slab
                if c in FOX_QK_GROUPS:
                    sqnorms.append(_max_head_sqnorms(slab.astype(jnp.float32)))
        gate_ref[rows, :] = jnp.dot(xn, wgate_ref[...],
                                    preferred_element_type=jnp.float32)
        n0 = (r // PROJ_SUB) * NSQ_ROWS
        nsq_ref[n0:n0 + NSQ_ROWS, :] = jnp.concatenate(sqnorms, axis=0)


def _in_proj(x2d, pre_g, w_in, layer, w_gate, rope, seq):
    t = x2d.shape[0]
    tiles_per_seq = seq // PROJ_TM
    row = lambda w: pl.BlockSpec((PROJ_TM, w), lambda i: (i, 0))
    table = pl.BlockSpec((PROJ_TM, LANES), lambda i: (i % tiles_per_seq, 0))
    w_spec = pl.BlockSpec((1,) + w_in.shape[1:], lambda i: (layer, 0, 0),
                          pipeline_mode=pl.Buffered(1))
    nsq_rows = PROJ_TM // PROJ_SUB * NSQ_ROWS
    return pl.pallas_call(
        _in_proj_kernel,
        grid=(t // PROJ_TM,),
        in_specs=[row(D_MODEL), _const_spec((1, D_MODEL)), w_spec,
                  _const_spec((D_MODEL, LANES)), table, table, table],
        scratch_shapes=[pltpu.VMEM((D_MODEL, PROJ_WIDTH), jnp.bfloat16)],
        out_specs=[row(PROJ_WIDTH), row(LANES),
                   pl.BlockSpec((nsq_rows, LANES), lambda i: (i, 0))],
        out_shape=[jax.ShapeDtypeStruct((t, PROJ_WIDTH), jnp.bfloat16),
                   jax.ShapeDtypeStruct((t, LANES), jnp.float32),
                   jax.ShapeDtypeStruct((t // PROJ_TM * nsq_rows, LANES),
                                        jnp.float32)],
        compiler_params=pltpu.CompilerParams(
            dimension_semantics=("arbitrary",),
            vmem_limit_bytes=VMEM_LIMIT_BYTES),
        name="in_proj",
    )(x2d, pre_g, w_in, w_gate, *rope)


C_PIECES = 3
C_LANES = N_FOX_HEADS * C_PIECES


def _decay_kernel(gate_ref, b_ref, piece_ref, ccol_ref, crow_ref, cpiece_ref):
    z = gate_ref[0] + b_ref[...]
    c = jnp.minimum(z, 0.0) - jnp.log1p(jnp.exp(-jnp.abs(z)))
    seq = c.shape[0]
    rows = lax.broadcasted_iota(jnp.int32, c.shape, 0)
    shift = 1
    while shift < seq:
        c = c + jnp.where(rows >= shift, pltpu.roll(c, shift, 0), 0.0)
        shift *= 2
    c = c * LOG2E
    ccol_ref[0] = c
    crow_ref[0] = c.T[:C_LANES, :]
    hi = c.astype(jnp.bfloat16)
    rest = c - hi.astype(jnp.float32)
    mid = rest.astype(jnp.bfloat16)
    lo = (rest - mid.astype(jnp.float32)).astype(jnp.bfloat16)
    piece = piece_ref[...]
    cpiece_ref[0] = jnp.where(piece == 0, hi, jnp.where(piece == 1, mid, lo))


def _decay(gate, bias_row, piece_row, batch, seq):
    gate3 = gate.reshape(batch, seq, LANES)
    col = pl.BlockSpec((1, seq, LANES), lambda b: (b, 0, 0))
    row = pl.BlockSpec((1, C_LANES, seq), lambda b: (b, 0, 0))
    return pl.pallas_call(
        _decay_kernel,
        grid=(batch,),
        in_specs=[col, _const_spec((1, LANES)), _const_spec((1, LANES))],
        out_specs=[col, row, col],
        out_shape=[jax.ShapeDtypeStruct((batch, seq, LANES), jnp.float32),
                   jax.ShapeDtypeStruct((batch, C_LANES, seq), jnp.float32),
                   jax.ShapeDtypeStruct((batch, seq, LANES), jnp.bfloat16)],
        compiler_params=pltpu.CompilerParams(
            dimension_semantics=("arbitrary",),
            vmem_limit_bytes=VMEM_LIMIT_BYTES),
        name="decay",
    )(gate3, bias_row, piece_row)


ATT_TK = 256
ATT_TQ = (2 * ATT_TK, 4 * ATT_TK)
LOGITS_LEAD = 1
ONES_ROWS = 16
VT_ROWS = PAIR + ONES_ROWS
SKIP_LOG2 = 160.0
BOUND_SLACK = 1.02


def _logit_bound(nsq_ref):
    n = nsq_ref[0:NSQ_ROWS, :]
    for r in range(NSQ_ROWS, nsq_ref.shape[0], NSQ_ROWS):
        n = jnp.maximum(n, nsq_ref[r:r + NSQ_ROWS, :])
    bound = jnp.sqrt(n[:UNITS] * n[UNITS:]) * BOUND_SLACK
    lane = lax.broadcasted_iota(jnp.int32, (1, LANES), 1)
    out = jnp.zeros((1, LANES), jnp.float32)
    for h in range(N_FOX_HEADS):
        value = jnp.sum(jnp.where(lane == h % 2, bound[h // 2:h // 2 + 1], 0.0),
                        axis=1, keepdims=True)
        mine = (lane >= C_PIECES * h) & (lane < C_PIECES * (h + 1))
        out = jnp.where(mine, value, out)
    return out


def _first_needed_block(ccol_ref, bh_scr, q0, tk, limit):
    n_blocks = ccol_ref.shape[1] // tk
    c_end = ccol_ref[0, pl.ds(tk - 1, n_blocks, stride=tk), :]
    c_q = ccol_ref[0, pl.ds(q0, 1), :]
    bound = 2.0 * bh_scr[...] + (c_q - c_end)
    block = lax.broadcasted_iota(jnp.int32, bound.shape, 0)
    lane = lax.broadcasted_iota(jnp.int32, bound.shape, 1)
    needed = (lane < C_LANES) & jnp.logical_not(bound < -SKIP_LOG2)
    first = jnp.min(jnp.where(needed, block, n_blocks).astype(jnp.float32))
    return jnp.minimum(first.astype(jnp.int32), limit)


def _decay_selector(u, width, th):
    row = lax.broadcasted_iota(jnp.int32, (LANES, width), 0)
    col = lax.broadcasted_iota(jnp.int32, (LANES, width), 1)
    second_map = (col & th) != 0
    first = C_PIECES * 2 * u
    owner_a = (row >= first) & (row < first + C_PIECES)
    owner_b = (row >= first + C_PIECES) & (row < first + 2 * C_PIECES)
    hit = (second_map & owner_b) | (jnp.logical_not(second_map) & owner_a)
    return jnp.where(hit, -1.0, 0.0).astype(jnp.bfloat16)


def _attn_body(is_fox, q_ref, k_ref, v_ref, ccol_ref, crow_ref, cpiece_ref,
               nsq_ref, lam_ref, g_ref, o_ref, m_scr, l_scr, acc_scr, qt_scr,
               s0_buf, s1_buf, max0_scr, max1_scr, vt_scr, bh_scr):
    s0_scr, s1_scr = (s0_buf, max0_scr), (s1_buf, max1_scr)
    tq = q_ref.shape[0]
    th = tk = ATT_TK
    n_parts = tq // th
    qi = pl.program_id(1)
    q0 = pl.multiple_of(qi * tq, tq)

    @pl.when(qi == 0)
    def _():
        seq = v_ref.shape[0]
        for u in range(UNITS):
            vt_scr[u, :PAIR, :] = v_ref[:, u * PAIR:(u + 1) * PAIR].T
            vt_scr[u, PAIR:, :] = jnp.ones((ONES_ROWS, seq), vt_scr.dtype)
            if is_fox:
                qt_scr[u, PAIR:, :] = _decay_selector(u, 2 * tq, th)
        if is_fox:
            bh_scr[...] = _logit_bound(nsq_ref)

    first_diag = n_parts * qi
    j0 = (_first_needed_block(ccol_ref, bh_scr, q0, tk, first_diag)
          if is_fox else 0)
    n_full = first_diag - j0

    lane = lax.broadcasted_iota(jnp.int32, (th, LANES), 1)
    ct = []
    for u in range(UNITS):
        pieces, ct_u = [], []
        for h in range(n_parts):
            q = q_ref[h * th:(h + 1) * th, u * PAIR:(u + 1) * PAIR]
            zero = jnp.zeros_like(q)
            pieces += [jnp.where(lane < HEAD_DIM, q, zero),
                       jnp.where(lane >= HEAD_DIM, q, zero)]
            if is_fox:
                t0 = pl.multiple_of(q0 + h * th, th)
                ct_u += [crow_ref[0, pl.ds(C_PIECES * (2 * u + m), 1),
                                  pl.ds(t0, th)] for m in range(2)]
        qt_scr[u, :PAIR, :] = jnp.concatenate(pieces, axis=0).T
        if is_fox:
            ct.append(jnp.concatenate(ct_u, axis=1))

    m_scr[...] = jnp.full_like(m_scr, MASK_VALUE)
    l_scr[...] = jnp.zeros_like(l_scr)
    acc_scr[...] = jnp.zeros_like(acc_scr)

    def logits_to(slot, j, u, lo=0, hi=None):
        k0 = pl.multiple_of(j * tk, tk)
        kb = k_ref[pl.ds(k0, tk), u * PAIR:(u + 1) * PAIR]
        if is_fox:
            kb = jnp.concatenate([kb, cpiece_ref[0, pl.ds(k0, tk), :]], axis=1)
        s = jnp.dot(kb, qt_scr[u, :, lo:hi], preferred_element_type=jnp.float32)
        buf, col_max = slot
        buf[u, :, lo:hi] = s
        col_max[u, :, lo:hi] = jnp.max(s, axis=0, keepdims=True)

    def softmax_pv(slot, j, u, lo, hi, masked):
        k0 = pl.multiple_of(j * tk, tk)
        buf, col_max_ref = slot
        s = buf[u, :, lo:hi]
        width = s.shape[1]
        if masked:
            r = lax.broadcasted_iota(jnp.int32, (tk, width), 0)
            c = lax.broadcasted_iota(jnp.int32, (tk, width), 1)
            keep = r <= (c & (th - 1))
            if width > 2 * th:
                keep = keep | (c >= 2 * th)
            s = jnp.where(keep, s, MASK_VALUE)
            col_max = jnp.max(s, axis=0, keepdims=True)
        else:
            col_max = col_max_ref[u, :, lo:hi]
        m_prev = m_scr[u, :, lo:hi]
        if is_fox:
            m_next = jnp.maximum(m_prev, col_max + ct[u][:, lo:hi])
            offset = m_next - ct[u][:, lo:hi]
        else:
            m_next = jnp.maximum(m_prev, col_max)
            offset = m_next
        p = jnp.exp2(s - offset).astype(vt_scr.dtype)
        alpha = jnp.exp2(m_prev - m_next)
        vt = vt_scr[u, :, pl.ds(k0, tk)]
        pv = jnp.dot(vt, p, preferred_element_type=jnp.float32)
        l_scr[u, :, lo:hi] = alpha * l_scr[u, :, lo:hi] + pv[PAIR:PAIR + 1, :]
        acc_scr[u, :, lo:hi] = alpha * acc_scr[u, :, lo:hi] + pv[:PAIR, :]
        m_scr[u, :, lo:hi] = m_next

    def step(j, cur, nxt, lo=0, hi=None, masked=False, nxt_lo=0):
        ahead = LOGITS_LEAD
        if nxt is not None:
            for u in range(ahead):
                logits_to(nxt, j + 1, u, nxt_lo)
        for u in range(UNITS):
            softmax_pv(cur, j, u, lo, hi, masked)
            if nxt is not None and u + ahead < UNITS:
                logits_to(nxt, j + 1, u + ahead, nxt_lo)

    if is_fox:
        j1 = _first_needed_block(ccol_ref, bh_scr, q0 + th, tk, first_diag)
        narrow = (n_full == 1) & (j1 == first_diag)
        wide = jnp.logical_not(narrow)

        @pl.when(narrow)
        def _():
            for u in range(UNITS):
                logits_to(s0_scr, j0, u, hi=2 * th)

        @pl.when(wide)
        def _():
            for u in range(UNITS):
                logits_to(s0_scr, j0, u)
    else:
        for u in range(UNITS):
            logits_to(s0_scr, j0, u)

    def pair(i, carry):
        step(j0 + 2 * i, s0_scr, s1_scr)
        step(j0 + 2 * i + 1, s1_scr, s0_scr)
        return carry

    lax.fori_loop(0, lax.shift_right_logical(n_full, 1), pair, 0)

    def diagonal(cur, other):
        for k in range(n_parts):
            nxt = other if k + 1 < n_parts else None
            step(first_diag + k, cur, nxt, lo=2 * th * k, masked=True,
                 nxt_lo=2 * th * (k + 1))
            cur, other = other, cur

    if is_fox:
        @pl.when(narrow)
        def _():
            step(first_diag - 1, s0_scr, s1_scr, hi=2 * th)
            diagonal(s1_scr, s0_scr)

        @pl.when(wide & ((n_full & 1) == 0))
        def _():
            diagonal(s0_scr, s1_scr)

        @pl.when(wide & ((n_full & 1) == 1))
        def _():
            step(first_diag - 1, s0_scr, s1_scr)
            diagonal(s1_scr, s0_scr)
    else:
        diagonal(s0_scr, s1_scr)

    if not is_fox:
        lam_rows = lam_ref[...]
        dot = lambda i: jnp.sum(lam_rows[i:i + 1] * lam_rows[i + 1:i + 2],
                                axis=1, keepdims=True)
        lam = jnp.exp(dot(0)) - jnp.exp(dot(2)) + LAMBDA_INIT
    for u in range(UNITS):
        o_t = acc_scr[u] / l_scr[u]
        for h in range(n_parts):
            a = o_t[:, 2 * h * th:(2 * h + 1) * th]
            b = o_t[:, (2 * h + 1) * th:(2 * h + 2) * th]
            if is_fox:
                out = jnp.concatenate([a[:HEAD_DIM], b[HEAD_DIM:]], axis=0).T
            else:
                d_t = a - lam * b
                inv = lax.rsqrt(jnp.mean(d_t * d_t, axis=0, keepdims=True)
                                + RMS_EPS)
                out = (d_t * inv).T * g_ref[...] * (1.0 - LAMBDA_INIT)
            o_ref[h * th:(h + 1) * th, u * PAIR:(u + 1) * PAIR] = out.astype(
                o_ref.dtype)


def _fox_kernel(q_ref, k_ref, v_ref, ccol_ref, crow_ref, cpiece_ref, nsq_ref,
                o_ref, *scratch):
    _attn_body(True, q_ref, k_ref, v_ref, ccol_ref, crow_ref, cpiece_ref,
               nsq_ref, None, None, o_ref, *scratch)


def _diff_kernel(q_ref, k_ref, v_ref, lam_ref, g_ref, o_ref, *scratch):
    _attn_body(False, q_ref, k_ref, v_ref, None, None, None, None, lam_ref,
               g_ref, o_ref, *scratch, None)


def _attention(proj, batch, seq, group, kernel, extra_specs, extra_args,
               extra_scratch, name):
    tq = ATT_TQ[group]
    nq = seq // tq
    q_spec = pl.BlockSpec((tq, GROUP), lambda b, i: (b * nq + i, group))
    kv_spec = lambda g: pl.BlockSpec((seq, GROUP), lambda b, i: (b, g))
    out_spec = pl.BlockSpec((tq, GROUP), lambda b, i: (b * nq + i, 0))
    stat = pltpu.VMEM((UNITS, 1, 2 * tq), jnp.float32)
    acc = pltpu.VMEM((UNITS, PAIR, 2 * tq), jnp.float32)
    depth = PAIR + (LANES if group == 0 else 0)
    q_t = pltpu.VMEM((UNITS, depth, 2 * tq), jnp.bfloat16)
    logit_slot = pltpu.VMEM((UNITS, ATT_TK, 2 * tq), jnp.float32)
    v_t = pltpu.VMEM((UNITS, VT_ROWS, seq), jnp.bfloat16)
    return pl.pallas_call(
        kernel,
        grid=(batch, nq),
        in_specs=[q_spec, kv_spec(2 + group), kv_spec(4 + group)] + extra_specs,
        out_specs=out_spec,
        out_shape=jax.ShapeDtypeStruct((batch * seq, GROUP), jnp.bfloat16),
        scratch_shapes=[stat, stat, acc, q_t, logit_slot, logit_slot, stat,
                        stat, v_t]
        + extra_scratch,
        compiler_params=pltpu.CompilerParams(
            dimension_semantics=("arbitrary", "arbitrary"),
            vmem_limit_bytes=VMEM_LIMIT_BYTES),
        name=name,
    )(proj, proj, proj, *extra_args)


def _rope_tables(seq):
    inv_freq = ROPE_THETA ** (-jnp.arange(0, ROPE_DIM, 2, dtype=jnp.float32)
                              / ROPE_DIM)
    p = np.arange(LANES) % HEAD_DIM
    lane_freq = jnp.where(p < ROPE_DIM, inv_freq[p % ROPE_HALF], 0.0)
    ang = jnp.arange(seq).astype(jnp.float32)[:, None] * lane_freq[None, :]
    cos, sin = jnp.cos(ang), jnp.sin(ang)
    sin_up = jnp.where((p >= ROPE_HALF) & (p < ROPE_DIM), sin, 0.0)
    sin_dn = jnp.where(p < ROPE_HALF, -sin, 0.0)
    return cos, sin_up, sin_dn


def _decay_lanes(per_head):
    rep = jnp.repeat(per_head, C_PIECES, axis=-1)
    pad = [(0, 0)] * (rep.ndim - 1) + [(0, LANES - C_LANES)]
    return jnp.pad(rep, pad)


def _gate_weight(w_in):
    gate_start = 3 * FOX_WIDTH
    fgate = w_in[:, gate_start:gate_start + N_FOX_HEADS]
    return _decay_lanes(fgate).astype(jnp.bfloat16)


def kernel(x, ffn1_pre_g, ffn1_post_g, ffn1_w_gate, ffn1_w_up, ffn1_w_down, mix_pre_g, mix_post_g, w_in, fox_forget_b, diff_lambda_q1, diff_lambda_k1, diff_lambda_q2, diff_lambda_k2, diff_subln_g, w_out, ffn2_pre_g, ffn2_post_g, ffn2_w_gate, ffn2_w_up, ffn2_w_down):
    batch, seq, d = x.shape
    bf = lambda w: w.astype(jnp.bfloat16)
    x2d = x.reshape(batch * seq, d)
    rope = _rope_tables(seq)
    for l in range(ffn1_pre_g.shape[0]):
        x2d = _ffn(x2d, ffn1_pre_g[l][None], ffn1_post_g[l][None],
                   bf(ffn1_w_gate[l]), bf(ffn1_w_up[l]), bf(ffn1_w_down[l]))

        proj, gate, nsq = _in_proj(x2d, mix_pre_g[l][None], w_in, l,
                                   _gate_weight(w_in[l]), rope, seq)
        piece_row = jnp.asarray(np.arange(LANES) % C_PIECES, jnp.int32)[None]
        ccol, crow, cpiece = _decay(gate, _decay_lanes(fox_forget_b[l])[None],
                                    piece_row, batch, seq)

        per_batch_col = pl.BlockSpec((1, seq, LANES), lambda b, i: (b, 0, 0))
        fox_o = _attention(
            proj, batch, seq, 0, _fox_kernel,
            [per_batch_col,
             pl.BlockSpec((1, C_LANES, seq), lambda b, i: (b, 0, 0)),
             per_batch_col,
             pl.BlockSpec((seq // PROJ_SUB * NSQ_ROWS, LANES),
                          lambda b, i: (b, 0))],
            [ccol, crow, cpiece, nsq],
            [pltpu.VMEM((1, LANES), jnp.float32)], "fox_attn")
        lam_rows = jnp.pad(
            jnp.stack([diff_lambda_q1[l], diff_lambda_k1[l],
                       diff_lambda_q2[l], diff_lambda_k2[l]]),
            ((0, 4), (0, LANES - HEAD_DIM)))
        diff_o = _attention(
            proj, batch, seq, 1, _diff_kernel,
            [_const_spec((8, LANES)), _const_spec((1, PAIR))],
            [lam_rows, diff_subln_g[l][None]], [], "diff_attn")

        x2d = _mix_ffn(x2d, fox_o, diff_o, bf(w_out[l]), mix_post_g[l][None],
                       ffn2_pre_g[l][None], ffn2_post_g[l][None],
                       bf(ffn2_w_gate[l]), bf(ffn2_w_up[l]), bf(ffn2_w_down[l]))
    return x2d.reshape(batch, seq, d)
```
